```python
import jax, jax.numpy as jnp
from jax import lax
import numpy as np

D_MODEL = 1024
BATCH = 4
SEQ = 4096
DEPTH = 2

CHUNK = 64
N_A_LAYERS = DEPTH // 2
N_B_LAYERS = DEPTH - N_A_LAYERS
D_FF = 2816
GLA_HEADS = 4
GLA_DK = D_MODEL // 2
GLA_DV = D_MODEL
GLA_HEAD_K = GLA_DK // GLA_HEADS
GLA_HEAD_V = GLA_DV // GLA_HEADS
GATE_RANK = 16
GATE_TAU = 16.0
SB_HEADS = 16
SB_DIM = D_MODEL
SB_HEAD = SB_DIM // SB_HEADS
Q_BLOCK = 128
DEEPNORM_ALPHA = (2 * DEPTH) ** 0.25
DEEPNORM_BETA = (8 * DEPTH) ** -0.25
LN_EPS = 1e-5
RMS_EPS = 1e-6

kernel_name = 'yoco_gla_stickbreak_macaron_deepnorm'


def layer_norm(x, g, b):
    xf = x.astype(jnp.float32)
    mu = jnp.mean(xf, axis=-1, keepdims=True)
    var = jnp.mean(jnp.square(xf - mu), axis=-1, keepdims=True)
    y = (xf - mu) * lax.rsqrt(var + LN_EPS) * g.astype(jnp.float32) + b.astype(jnp.float32)
    return y.astype(x.dtype)


def swiglu(x, w_up, w_down):
    gate, up = jnp.split(x @ w_up, 2, axis=-1)
    return (jax.nn.silu(gate) * up) @ w_down


def gla_mixer(x, w_in, w_gk, b_gk, norm_g, w_out):
    bsz, seq, _ = x.shape
    nc = seq // CHUNK
    f32 = jnp.float32
    proj = x @ w_in
    q, k, v, r, low = jnp.split(
        proj, [GLA_DK, 2 * GLA_DK, 2 * GLA_DK + GLA_DV, 2 * GLA_DK + 2 * GLA_DV], axis=-1)
    log_g = jax.nn.log_sigmoid((low @ w_gk + b_gk).astype(f32)) / GATE_TAU

    def chunked(t, hd):
        return t.astype(f32).reshape(bsz, nc, CHUNK, GLA_HEADS, hd)

    q = chunked(q, GLA_HEAD_K) * GLA_HEAD_K ** -0.5
    k = chunked(k, GLA_HEAD_K)
    v = chunked(v, GLA_HEAD_V)
    log_g = chunked(log_g, GLA_HEAD_K)
    b_cum = jnp.cumsum(log_g, axis=2)
    b_tot = b_cum[:, :, -1]
    k_dec = k * jnp.exp(b_tot[:, :, None] - b_cum)
    scores = jnp.einsum('bnihd,bnjhd->bnhij', q, k_dec)
    o_intra = jnp.einsum('bnhij,bnjhe->bnihe', scores, v)
    u = jnp.einsum('bnjhd,bnjhe->nbhde', k_dec, v)
    decay = jnp.moveaxis(jnp.exp(b_tot), 1, 0)

    def step(s, inp):
        dec, uc = inp
        return dec[..., None] * s + uc, s

    s0 = jnp.zeros((bsz, GLA_HEADS, GLA_HEAD_K, GLA_HEAD_V), f32)
    _, s_prev = lax.scan(step, s0, (decay, u))
    o_inter = jnp.einsum('bnihd,nbhde->bnihe', q * jnp.exp(b_tot)[:, :, None], s_prev)
    o = o_intra + o_inter
    o = o * lax.rsqrt(jnp.mean(jnp.square(o), axis=-1, keepdims=True) + RMS_EPS) * norm_g.astype(f32)
    o = o.reshape(bsz, seq, GLA_DV).astype(x.dtype)
    return (jax.nn.silu(r) * o) @ w_out


def shared_kv(x, w_kv):
    bsz, seq, _ = x.shape
    k, v = jnp.split(x @ w_kv, 2, axis=-1)
    k = k.reshape(bsz, seq, SB_HEADS, SB_HEAD).transpose(0, 2, 1, 3)
    v = v.reshape(bsz, seq, SB_HEADS, SB_HEAD).transpose(0, 2, 1, 3)
    return k, v


def stick_breaking(x, k, v, w_q, w_out):
    bsz, seq, _ = x.shape
    nb = seq // Q_BLOCK
    q = (x @ w_q).reshape(bsz, nb, Q_BLOCK, SB_HEADS, SB_HEAD).transpose(1, 0, 3, 2, 4)
    s_pos = jnp.arange(seq)
    scale = SB_HEAD ** -0.5

    def block(inp):
        qb, start = inp
        z = jnp.einsum('bhid,bhjd->bhij', qb, k, preferred_element_type=jnp.float32) * scale
        t_pos = start + jnp.arange(Q_BLOCK)
        mask = s_pos[None, :] < t_pos[:, None]
        log_keep = jnp.where(mask, jax.nn.log_sigmoid(-z), 0.0)
        log_a = jax.nn.log_sigmoid(z) + lax.cumsum(log_keep, axis=3, reverse=True) - log_keep
        a = jnp.where(mask, jnp.exp(log_a), 0.0)
        return jnp.einsum('bhij,bhjd->bhid', a.astype(v.dtype), v)

    o = lax.map(block, (q, jnp.arange(nb) * Q_BLOCK))
    o = o.transpose(1, 0, 3, 2, 4).reshape(bsz, seq, SB_DIM)
    return o @ w_out


def setup_inputs(seed: int = 0) -> dict:
    key = jax.random.key(seed)
    ks = jax.random.split(key, 20)

    def nrm(k, shape, scale):
        return jax.random.normal(k, shape, jnp.float32) * scale

    ds = D_MODEL ** -0.5
    x = nrm(ks[0], (BATCH, SEQ, D_MODEL), 1.0)
    ln_g = 1.0 + nrm(ks[1], (DEPTH, 3, D_MODEL), 0.02)
    ln_b = nrm(ks[2], (DEPTH, 3, D_MODEL), 0.02)
    ffn_w_up = nrm(ks[3], (DEPTH, 2, D_MODEL, 2 * D_FF), ds)
    ffn_w_down = nrm(ks[4], (DEPTH, 2, D_FF, D_MODEL), D_FF ** -0.5 * DEEPNORM_BETA)
    gla_w_in = jnp.concatenate([
        nrm(ks[5], (N_A_LAYERS, D_MODEL, 2 * GLA_DK), ds),
        nrm(ks[6], (N_A_LAYERS, D_MODEL, GLA_DV), ds * DEEPNORM_BETA),
        nrm(ks[7], (N_A_LAYERS, D_MODEL, GLA_DV), ds),
        nrm(ks[8], (N_A_LAYERS, D_MODEL, GATE_RANK), ds),
    ], axis=-1)
    gla_w_gk = nrm(ks[9], (N_A_LAYERS, GATE_RANK, GLA_DK), GATE_RANK ** -0.5)
    gla_b_gk = nrm(ks[10], (N_A_LAYERS, GLA_DK), 0.01)
    gla_norm_g = 1.0 + nrm(ks[11], (N_A_LAYERS, GLA_HEAD_V), 0.02)
    gla_w_out = nrm(ks[12], (N_A_LAYERS, GLA_DV, D_MODEL), GLA_DV ** -0.5 * DEEPNORM_BETA)
    sb_w_kv = jnp.concatenate([
        nrm(ks[13], (D_MODEL, SB_DIM), ds),
        nrm(ks[14], (D_MODEL, SB_DIM), ds * DEEPNORM_BETA),
    ], axis=-1)
    sb_w_q = nrm(ks[15], (N_B_LAYERS, D_MODEL, SB_DIM), ds)
    sb_w_out = nrm(ks[16], (N_B_LAYERS, SB_DIM, D_MODEL), SB_DIM ** -0.5 * DEEPNORM_BETA)
    return {'x': x, 'ln_g': ln_g, 'ln_b': ln_b, 'ffn_w_up': ffn_w_up, 'ffn_w_down': ffn_w_down,
            'gla_w_in': gla_w_in, 'gla_w_gk': gla_w_gk, 'gla_b_gk': gla_b_gk,
            'gla_norm_g': gla_norm_g, 'gla_w_out': gla_w_out,
            'sb_w_kv': sb_w_kv, 'sb_w_q': sb_w_q, 'sb_w_out': sb_w_out}


def reference(x, ln_g, ln_b, ffn_w_up, ffn_w_down, gla_w_in, gla_w_gk, gla_b_gk,
              gla_norm_g, gla_w_out, sb_w_kv, sb_w_q, sb_w_out):
    k_sh, v_sh = None, None
    for layer in range(DEPTH):
        if layer == N_A_LAYERS:
            k_sh, v_sh = shared_kv(x, sb_w_kv)
        x = layer_norm(DEEPNORM_ALPHA * x + 0.5 * swiglu(x, ffn_w_up[layer, 0], ffn_w_down[layer, 0]),
                       ln_g[layer, 0], ln_b[layer, 0])
        if layer < N_A_LAYERS:
            mix = gla_mixer(x, gla_w_in[layer], gla_w_gk[layer], gla_b_gk[layer],
                            gla_norm_g[layer], gla_w_out[layer])
        else:
            j = layer - N_A_LAYERS
            mix = stick_breaking(x, k_sh, v_sh, sb_w_q[j], sb_w_out[j])
        x = layer_norm(DEEPNORM_ALPHA * x + mix, ln_g[layer, 1], ln_b[layer, 1])
        x = layer_norm(DEEPNORM_ALPHA * x + 0.5 * swiglu(x, ffn_w_up[layer, 1], ffn_w_down[layer, 1]),
                       ln_g[layer, 2], ln_b[layer, 2])
    return x
```

```python
import functools

import jax
import jax.numpy as jnp
from jax import lax
from jax.experimental import pallas as pl
from jax.experimental.pallas import tpu as pltpu

F32 = jnp.float32
BF16 = jnp.bfloat16

D_MODEL = 1024
DEPTH = 2
CHUNK = 64
N_A_LAYERS = DEPTH // 2
D_FF = 2816
GLA_HEADS = 4
GLA_DK = D_MODEL // 2
GLA_DV = D_MODEL
GLA_HEAD_K = GLA_DK // GLA_HEADS
GLA_HEAD_V = GLA_DV // GLA_HEADS
GATE_RANK = 16
GATE_TAU = 16.0
SB_HEADS = 16
SB_HEAD = D_MODEL // SB_HEADS
DEEPNORM_ALPHA = (2 * DEPTH) ** 0.25
LN_EPS = 1e-5
RMS_EPS = 1e-6

LANES = 128
VMEM_LIMIT = 48 * 1024 * 1024

FFN_TM = 1024
FFN_TF = 256
PROJ_TM = 512
GLA_RB = 256
SB_QB = 128
SB_KB = 128
SB_LOG_ZERO = -104.0


def _layer_norm(y, g, b):
    mu = jnp.mean(y, axis=-1, keepdims=True)
    yc = y - mu
    var = jnp.mean(yc * yc, axis=-1, keepdims=True)
    return yc * lax.rsqrt(var + LN_EPS) * g + b


def _silu(x):
    return x * jax.nn.sigmoid(x)


def _ffn_kernel(x_ref, wg_ref, wu_ref, wd_ref, g_ref, b_ref, o_ref, xb_ref, acc_ref):
    f = pl.program_id(1)

    @pl.when(f == 0)
    def _():
        xb_ref[...] = x_ref[...].astype(BF16)
        acc_ref[...] = jnp.zeros_like(acc_ref)

    xb = xb_ref[...]
    gate = jnp.dot(xb, wg_ref[...], preferred_element_type=F32)
    up = jnp.dot(xb, wu_ref[...], preferred_element_type=F32)
    act = (_silu(gate) * up).astype(BF16)
    acc_ref[...] += jnp.dot(act, wd_ref[...], preferred_element_type=F32)

    @pl.when(f == pl.num_programs(1) - 1)
    def _():
        y = DEEPNORM_ALPHA * x_ref[...] + 0.5 * acc_ref[...]
        o_ref[...] = _layer_norm(y, g_ref[...], b_ref[...])


def _ffn_ln(x, w_up, w_down, g, b):
    m, d = x.shape
    nf = D_FF // FFN_TF
    return pl.pallas_call(
        _ffn_kernel,
        grid=(m // FFN_TM, nf),
        in_specs=[
            pl.BlockSpec((FFN_TM, d), lambda i, f: (i, 0)),
            pl.BlockSpec((d, FFN_TF), lambda i, f: (0, f)),
            pl.BlockSpec((d, FFN_TF), lambda i, f: (0, f + nf)),
            pl.BlockSpec((FFN_TF, d), lambda i, f: (f, 0)),
            pl.BlockSpec((1, d), lambda i, f: (0, 0)),
            pl.BlockSpec((1, d), lambda i, f: (0, 0)),
        ],
        out_specs=pl.BlockSpec((FFN_TM, d), lambda i, f: (i, 0)),
        out_shape=jax.ShapeDtypeStruct((m, d), F32),
        scratch_shapes=[pltpu.VMEM((FFN_TM, d), BF16), pltpu.VMEM((FFN_TM, d), F32)],
        compiler_params=pltpu.CompilerParams(
            dimension_semantics=("parallel", "arbitrary"), vmem_limit_bytes=VMEM_LIMIT),
        name="ffn_ln",
    )(x, w_up, w_up, w_down, g.reshape(1, d), b.reshape(1, d))


def _matmul_kernel(x_ref, w_ref, o_ref):
    o_ref[...] = jnp.dot(x_ref[...].astype(BF16), w_ref[...],
                         preferred_element_type=F32).astype(o_ref.dtype)


def _matmul(x, w, out_dtype):
    m, d = x.shape
    n = w.shape[1]
    return pl.pallas_call(
        _matmul_kernel,
        grid=(m // PROJ_TM,),
        in_specs=[pl.BlockSpec((PROJ_TM, d), lambda i: (i, 0)),
                  pl.BlockSpec((d, n), lambda i: (0, 0))],
        out_specs=pl.BlockSpec((PROJ_TM, n), lambda i: (i, 0)),
        out_shape=jax.ShapeDtypeStruct((m, n), out_dtype),
        compiler_params=pltpu.CompilerParams(
            dimension_semantics=("parallel",), vmem_limit_bytes=VMEM_LIMIT),
        name="matmul",
    )(x, w)


def _proj_ln_kernel(y_ref, x_ref, w_ref, g_ref, b_ref, o_ref):
    mix = jnp.dot(y_ref[...], w_ref[...], preferred_element_type=F32)
    o_ref[...] = _layer_norm(DEEPNORM_ALPHA * x_ref[...] + mix, g_ref[...], b_ref[...])


def _proj_ln(y, x, w, g, b):
    m, d = x.shape
    return pl.pallas_call(
        _proj_ln_kernel,
        grid=(m // PROJ_TM,),
        in_specs=[pl.BlockSpec((PROJ_TM, y.shape[1]), lambda i: (i, 0)),
                  pl.BlockSpec((PROJ_TM, d), lambda i: (i, 0)),
                  pl.BlockSpec(w.shape, lambda i: (0, 0)),
                  pl.BlockSpec((1, d), lambda i: (0, 0)),
                  pl.BlockSpec((1, d), lambda i: (0, 0))],
        out_specs=pl.BlockSpec((PROJ_TM, d), lambda i: (i, 0)),
        out_shape=jax.ShapeDtypeStruct((m, d), F32),
        compiler_params=pltpu.CompilerParams(
            dimension_semantics=("parallel",), vmem_limit_bytes=VMEM_LIMIT),
        name="proj_ln",
    )(y, x, w, g.reshape(1, d), b.reshape(1, d))


def _gla_proj_kernel(x_ref, wm_ref, wl_ref, wgk_ref, bgk_ref, p_ref, lg_ref):
    xb = x_ref[...].astype(BF16)
    p_ref[...] = jnp.dot(xb, wm_ref[...], preferred_element_type=F32)
    low = jnp.dot(xb, wl_ref[...], preferred_element_type=F32)
    pre = jnp.dot(low.astype(BF16), wgk_ref[...], preferred_element_type=F32) + bgk_ref[...]
    lg_ref[...] = jax.nn.log_sigmoid(pre) / GATE_TAU


def _gla_proj(x, w_main, w_low, w_gk, b_gk):
    m, d = x.shape
    n = w_main.shape[1]
    return pl.pallas_call(
        _gla_proj_kernel,
        grid=(m // PROJ_TM,),
        in_specs=[pl.BlockSpec((PROJ_TM, d), lambda i: (i, 0)),
                  pl.BlockSpec((d, n), lambda i: (0, 0)),
                  pl.BlockSpec((d, LANES), lambda i: (0, 0)),
                  pl.BlockSpec((LANES, GLA_DK), lambda i: (0, 0)),
                  pl.BlockSpec((1, GLA_DK), lambda i: (0, 0))],
        out_specs=[pl.BlockSpec((PROJ_TM, n), lambda i: (i, 0)),
                   pl.BlockSpec((PROJ_TM, GLA_DK), lambda i: (i, 0))],
        out_shape=[jax.ShapeDtypeStruct((m, n), F32),
                   jax.ShapeDtypeStruct((m, GLA_DK), F32)],
        compiler_params=pltpu.CompilerParams(
            dimension_semantics=("parallel",), vmem_limit_bytes=VMEM_LIMIT),
        name="gla_proj",
    )(x, w_main, w_low, w_gk, b_gk.reshape(1, GLA_DK))


def _gla_kernel(q_ref, k_ref, v_ref, r_ref, lg_ref, cm_ref, ng_ref, o_ref, st_ref):
    @pl.when(pl.program_id(1) == 0)
    def _():
        st_ref[...] = jnp.zeros_like(st_ref)

    rb = GLA_RB
    lg = lg_ref[0]
    lg_hi = lg.astype(BF16)
    lg_lo = (lg - lg_hi.astype(F32)).astype(BF16)
    sums = jnp.dot(cm_ref[...], jnp.concatenate([lg_hi, lg_lo], axis=0),
                   preferred_element_type=F32)
    e_end = sums[:rb]
    dec_tot = jnp.exp(sums[rb:])
    q = q_ref[0] * (GLA_HEAD_K ** -0.5)
    k_dec = (k_ref[0] * jnp.exp(e_end)).astype(BF16)
    q_intra = q.astype(BF16)
    q_inter = (q * dec_tot).astype(BF16)
    v = v_ref[0].astype(BF16)
    r = r_ref[0]
    ng = ng_ref[...]

    row_chunk = lax.broadcasted_iota(jnp.int32, (rb, rb), 0) // CHUNK
    col_chunk = lax.broadcasted_iota(jnp.int32, (rb, rb), 1) // CHUNK
    same_chunk = row_chunk == col_chunk

    for h in range(GLA_HEADS):
        ks = slice(h * GLA_HEAD_K, (h + 1) * GLA_HEAD_K)
        vs = slice(h * GLA_HEAD_V, (h + 1) * GLA_HEAD_V)
        qh, qih, kh, vh = q_intra[:, ks], q_inter[:, ks], k_dec[:, ks], v[:, vs]
        scores = lax.dot_general(qh, kh, (((1,), (1,)), ((), ())), preferred_element_type=F32)
        scores = jnp.where(same_chunk, scores, 0.0).astype(BF16)
        o_intra = jnp.dot(scores, vh, preferred_element_type=F32)
        outs = []
        for c in range(rb // CHUNK):
            rows = slice(c * CHUNK, (c + 1) * CHUNK)
            st = st_ref[h]
            o_inter = lax.dot_general(qih[rows], st.astype(BF16), (((1,), (1,)), ((), ())),
                                      preferred_element_type=F32)
            outs.append(o_intra[rows] + o_inter)
            upd = lax.dot_general(vh[rows], kh[rows], (((0,), (0,)), ((), ())),
                                  preferred_element_type=F32)
            st_ref[h] = st * dec_tot[c * CHUNK:c * CHUNK + 1, ks] + upd
        o = jnp.concatenate(outs, axis=0)
        o = o * lax.rsqrt(jnp.mean(o * o, axis=-1, keepdims=True) + RMS_EPS) * ng
        o_ref[0, :, vs] = (_silu(r[:, vs]) * o).astype(BF16)


def _gla_chunk_matrix():
    idx = jnp.arange(GLA_RB)
    same = (idx[:, None] // CHUNK) == (idx[None, :] // CHUNK)
    later = same & (idx[None, :] > idx[:, None])
    top = jnp.concatenate([later, later], axis=1)
    bot = jnp.concatenate([same, same], axis=1)
    return jnp.concatenate([top, bot], axis=0).astype(BF16)


def _gla_core(proj, log_g, norm_g, bsz, seq):
    proj3 = proj.reshape(bsz, seq, proj.shape[1])
    lg3 = log_g.reshape(bsz, seq, GLA_DK)
    rb = GLA_RB
    return pl.pallas_call(
        _gla_kernel,
        grid=(bsz, seq // rb),
        in_specs=[
            pl.BlockSpec((1, rb, GLA_DK), lambda b, i: (b, i, 0)),
            pl.BlockSpec((1, rb, GLA_DK), lambda b, i: (b, i, 1)),
            pl.BlockSpec((1, rb, GLA_DV), lambda b, i: (b, i, 1)),
            pl.BlockSpec((1, rb, GLA_DV), lambda b, i: (b, i, 2)),
            pl.BlockSpec((1, rb, GLA_DK), lambda b, i: (b, i, 0)),
            pl.BlockSpec((2 * rb, 2 * rb), lambda b, i: (0, 0)),
            pl.BlockSpec((1, GLA_HEAD_V), lambda b, i: (0, 0)),
        ],
        out_specs=pl.BlockSpec((1, rb, GLA_DV), lambda b, i: (b, i, 0)),
        out_shape=jax.ShapeDtypeStruct((bsz, seq, GLA_DV), BF16),
        scratch_shapes=[pltpu.VMEM((GLA_HEADS, GLA_HEAD_V, GLA_HEAD_K), F32)],
        compiler_params=pltpu.CompilerParams(
            dimension_semantics=("parallel", "arbitrary"), vmem_limit_bytes=VMEM_LIMIT),
        name="gla_core",
    )(proj3, proj3, proj3, proj3, lg3, _gla_chunk_matrix(), norm_g.reshape(1, GLA_HEAD_V))


def _sb_kernel(q_ref, k_ref, v_ref, t_ref, o_ref):
    i = pl.program_id(2)
    qb, kb = SB_QB, SB_KB
    q = q_ref[0]
    zero = jnp.zeros_like(q)
    head0 = lax.broadcasted_iota(jnp.int32, (qb, LANES), 1) < SB_HEAD
    q2 = jnp.concatenate([jnp.where(head0, q, zero), jnp.where(head0, zero, q)], axis=0)
    t_loc = lax.broadcasted_iota(jnp.int32, (2 * qb, kb), 0) % qb
    s_loc = lax.broadcasted_iota(jnp.int32, (2 * qb, kb), 1)
    causal = s_loc < t_loc
    tmat = t_ref[...]
    scale = SB_HEAD ** -0.5

    def block(j, run, acc, diagonal):
        start = pl.multiple_of(j * kb, kb)
        kblk = k_ref[0, pl.ds(start, kb), :]
        vblk = v_ref[0, pl.ds(start, kb), :]
        z = lax.dot_general(q2, kblk, (((1,), (1,)), ((), ())),
                            preferred_element_type=F32) * scale
        softplus = jnp.maximum(z, 0.0) + jnp.log1p(jnp.exp(-jnp.abs(z)))
        log_keep = -softplus
        log_beta = z - softplus
        if diagonal:
            log_keep = jnp.where(causal, log_keep, 0.0)
        lk_hi = log_keep.astype(BF16)
        lk_lo = (log_keep - lk_hi.astype(F32)).astype(BF16)
        sums = jnp.dot(jnp.concatenate([lk_hi, lk_lo], axis=1), tmat,
                       preferred_element_type=F32)
        log_a = log_beta + sums[:, :kb] + run
        a = jnp.exp(log_a)
        if diagonal:
            a = jnp.where(causal, a, 0.0)
        a = a.astype(BF16)
        a2 = jnp.concatenate([a[:qb], a[qb:]], axis=1)
        vzero = jnp.zeros_like(vblk)
        v2 = jnp.concatenate([jnp.where(head0, vblk, vzero), jnp.where(head0, vzero, vblk)], axis=0)
        acc = acc + jnp.dot(a2, v2, preferred_element_type=F32)
        return run + sums[:, kb:], acc

    run0 = jnp.zeros((2 * qb, kb), F32)
    acc0 = jnp.zeros((qb, LANES), F32)
    run1, acc1 = block(i, run0, acc0, True)

    def cond(carry):
        j, run, _ = carry
        return jnp.logical_and(j >= 0, jnp.max(run) > SB_LOG_ZERO)

    def body(carry):
        j, run, acc = carry
        run, acc = block(j, run, acc, False)
        return j - 1, run, acc

    _, _, acc = lax.while_loop(cond, body, (i - 1, run1, acc1))
    o_ref[0] = acc.astype(o_ref.dtype)


def _sb_sum_matrix():
    j = jnp.arange(SB_KB)
    later = j[:, None] > j[None, :]
    half = jnp.concatenate([later, jnp.ones((SB_KB, SB_KB), bool)], axis=1)
    return jnp.concatenate([half, half], axis=0).astype(BF16)


def _sb_attn(q, kv, bsz, seq):
    q3 = q.reshape(bsz, seq, D_MODEL)
    kv3 = kv.reshape(bsz, seq, 2 * D_MODEL)
    pairs = D_MODEL // LANES
    return pl.pallas_call(
        _sb_kernel,
        grid=(bsz, pairs, seq // SB_QB),
        in_specs=[
            pl.BlockSpec((1, SB_QB, LANES), lambda b, p, i: (b, i, p)),
            pl.BlockSpec((1, seq, LANES), lambda b, p, i: (b, 0, p)),
            pl.BlockSpec((1, seq, LANES), lambda b, p, i: (b, 0, pairs + p)),
            pl.BlockSpec((2 * SB_KB, 2 * SB_KB), lambda b, p, i: (0, 0)),
        ],
        out_specs=pl.BlockSpec((1, SB_QB, LANES), lambda b, p, i: (b, i, p)),
        out_shape=jax.ShapeDtypeStruct((bsz, seq, D_MODEL), BF16),
        compiler_params=pltpu.CompilerParams(
            dimension_semantics=("parallel", "parallel", "arbitrary"),
            vmem_limit_bytes=VMEM_LIMIT),
        name="sb_attn",
    )(q3, kv3, kv3, _sb_sum_matrix())


def kernel(x, ln_g, ln_b, ffn_w_up, ffn_w_down, gla_w_in, gla_w_gk, gla_b_gk, gla_norm_g,
           gla_w_out, sb_w_kv, sb_w_q, sb_w_out):
    bsz, seq, d = x.shape
    assert d == D_MODEL and seq % GLA_RB == 0 and seq % SB_QB == 0 and (bsz * seq) % FFN_TM == 0
    x = x.reshape(bsz * seq, d)
    w_up = ffn_w_up.astype(BF16)
    w_down = ffn_w_down.astype(BF16)

    kv = None
    for layer in range(DEPTH):
        if layer == N_A_LAYERS:
            kv = _matmul(x, sb_w_kv.astype(BF16), BF16)
        x = _ffn_ln(x, w_up[layer, 0], w_down[layer, 0], ln_g[layer, 0], ln_b[layer, 0])
        if layer < N_A_LAYERS:
            n_main = 2 * GLA_DK + 2 * GLA_DV
            w_in = gla_w_in[layer]
            w_low = jnp.pad(w_in[:, n_main:], ((0, 0), (0, LANES - GATE_RANK))).astype(BF16)
            w_gk = jnp.pad(gla_w_gk[layer], ((0, LANES - GATE_RANK), (0, 0))).astype(BF16)
            proj, log_g = _gla_proj(x, w_in[:, :n_main].astype(BF16), w_low, w_gk, gla_b_gk[layer])
            y = _gla_core(proj, log_g, gla_norm_g[layer], bsz, seq).reshape(bsz * seq, GLA_DV)
            w_out = gla_w_out[layer]
        else:
            j = layer - N_A_LAYERS
            q = _matmul(x, sb_w_q[j].astype(BF16), BF16)
            y = _sb_attn(q, kv, bsz, seq).reshape(bsz * seq, D_MODEL)
            w_out = sb_w_out[j]
        x = _proj_ln(y, x, w_out.astype(BF16), ln_g[layer, 1], ln_b[layer, 1])
        x = _ffn_ln(x, w_up[layer, 1], w_down[layer, 1], ln_g[layer, 2], ln_b[layer, 2])
    return x.reshape(bsz, seq, d)
```

```python
import functools

import jax
import jax.numpy as jnp
from jax import lax
from jax.experimental import pallas as pl
from jax.experimental.pallas import tpu as pltpu

F32 = jnp.float32
BF16 = jnp.bfloat16

D_MODEL = 1024
DEPTH = 2
CHUNK = 64
N_A_LAYERS = DEPTH // 2
D_FF = 2816
GLA_HEADS = 4
GLA_DK = D_MODEL // 2
GLA_DV = D_MODEL
GLA_HEAD_K = GLA_DK // GLA_HEADS
GLA_HEAD_V = GLA_DV // GLA_HEADS
GATE_RANK = 16
GATE_TAU = 16.0
SB_HEADS = 16
SB_HEAD = D_MODEL // SB_HEADS
DEEPNORM_ALPHA = (2 * DEPTH) ** 0.25
LN_EPS = 1e-5
RMS_EPS = 1e-6

LANES = 128
VMEM_LIMIT = 48 * 1024 * 1024

FFN_TM = 1024
FFN_TF = 256
PROJ_TM = 512
GLA_RB = 256
SB_QB = 128
SB_KB = 128
SB_WINDOW = 3
SB_PAIRS = 4
SB_LOG_ZERO = -104.0
LOG2E = 1.4426950408889634


def _layer_norm(y, g, b):
    mu = jnp.mean(y, axis=-1, keepdims=True)
    yc = y - mu
    var = jnp.mean(yc * yc, axis=-1, keepdims=True)
    return yc * lax.rsqrt(var + LN_EPS) * g + b


def _silu(x):
    return x * jax.nn.sigmoid(x)


def _ffn_kernel(x_ref, wg_ref, wu_ref, wd_ref, g_ref, b_ref, o_ref, xb_ref, acc_ref):
    f = pl.program_id(1)

    @pl.when(f == 0)
    def _():
        xb_ref[...] = x_ref[...].astype(BF16)
        acc_ref[...] = jnp.zeros_like(acc_ref)

    xb = xb_ref[...]
    gate = jnp.dot(xb, wg_ref[...], preferred_element_type=F32)
    up = jnp.dot(xb, wu_ref[...], preferred_element_type=F32)
    act = (_silu(gate) * up).astype(BF16)
    acc_ref[...] += jnp.dot(act, wd_ref[...], preferred_element_type=F32)

    @pl.when(f == pl.num_programs(1) - 1)
    def _():
        y = DEEPNORM_ALPHA * x_ref[...] + 0.5 * acc_ref[...]
        o_ref[...] = _layer_norm(y, g_ref[...], b_ref[...])


def _ffn_ln(x, w_up, w_down, g, b):
    m, d = x.shape
    nf = D_FF // FFN_TF
    return pl.pallas_call(
        _ffn_kernel,
        grid=(m // FFN_TM, nf),
        in_specs=[
            pl.BlockSpec((FFN_TM, d), lambda i, f: (i, 0)),
            pl.BlockSpec((d, FFN_TF), lambda i, f: (0, f)),
            pl.BlockSpec((d, FFN_TF), lambda i, f: (0, f + nf)),
            pl.BlockSpec((FFN_TF, d), lambda i, f: (f, 0)),
            pl.BlockSpec((1, d), lambda i, f: (0, 0)),
            pl.BlockSpec((1, d), lambda i, f: (0, 0)),
        ],
        out_specs=pl.BlockSpec((FFN_TM, d), lambda i, f: (i, 0)),
        out_shape=jax.ShapeDtypeStruct((m, d), F32),
        scratch_shapes=[pltpu.VMEM((FFN_TM, d), BF16), pltpu.VMEM((FFN_TM, d), F32)],
        compiler_params=pltpu.CompilerParams(
            dimension_semantics=("parallel", "arbitrary"), vmem_limit_bytes=VMEM_LIMIT),
        name="ffn_ln",
    )(x, w_up, w_up, w_down, g.reshape(1, d), b.reshape(1, d))


def _matmul_kernel(x_ref, w_ref, o_ref):
    o_ref[...] = jnp.dot(x_ref[...].astype(BF16), w_ref[...],
                         preferred_element_type=F32).astype(o_ref.dtype)


def _matmul(x, w, out_dtype):
    m, d = x.shape
    n = w.shape[1]
    return pl.pallas_call(
        _matmul_kernel,
        grid=(m // PROJ_TM,),
        in_specs=[pl.BlockSpec((PROJ_TM, d), lambda i: (i, 0)),
                  pl.BlockSpec((d, n), lambda i: (0, 0))],
        out_specs=pl.BlockSpec((PROJ_TM, n), lambda i: (i, 0)),
        out_shape=jax.ShapeDtypeStruct((m, n), out_dtype),
        compiler_params=pltpu.CompilerParams(
            dimension_semantics=("parallel",), vmem_limit_bytes=VMEM_LIMIT),
        name="matmul",
    )(x, w)


def _proj_ln_kernel(y_ref, x_ref, w_ref, g_ref, b_ref, o_ref):
    mix = jnp.dot(y_ref[...], w_ref[...], preferred_element_type=F32)
    o_ref[...] = _layer_norm(DEEPNORM_ALPHA * x_ref[...] + mix, g_ref[...], b_ref[...])


def _proj_ln(y, x, w, g, b):
    m, d = x.shape
    return pl.pallas_call(
        _proj_ln_kernel,
        grid=(m // PROJ_TM,),
        in_specs=[pl.BlockSpec((PROJ_TM, y.shape[1]), lambda i: (i, 0)),
                  pl.BlockSpec((PROJ_TM, d), lambda i: (i, 0)),
                  pl.BlockSpec(w.shape, lambda i: (0, 0)),
                  pl.BlockSpec((1, d), lambda i: (0, 0)),
                  pl.BlockSpec((1, d), lambda i: (0, 0))],
        out_specs=pl.BlockSpec((PROJ_TM, d), lambda i: (i, 0)),
        out_shape=jax.ShapeDtypeStruct((m, d), F32),
        compiler_params=pltpu.CompilerParams(
            dimension_semantics=("parallel",), vmem_limit_bytes=VMEM_LIMIT),
        name="proj_ln",
    )(y, x, w, g.reshape(1, d), b.reshape(1, d))


def _gla_proj_kernel(x_ref, wm_ref, wl_ref, wgk_ref, bgk_ref, p_ref, lg_ref):
    xb = x_ref[...].astype(BF16)
    p_ref[...] = jnp.dot(xb, wm_ref[...], preferred_element_type=F32)
    low = jnp.dot(xb, wl_ref[...], preferred_element_type=F32)
    pre = jnp.dot(low.astype(BF16), wgk_ref[...], preferred_element_type=F32) + bgk_ref[...]
    lg_ref[...] = jax.nn.log_sigmoid(pre) / GATE_TAU


def _gla_proj(x, w_main, w_low, w_gk, b_gk):
    m, d = x.shape
    n = w_main.shape[1]
    return pl.pallas_call(
        _gla_proj_kernel,
        grid=(m // PROJ_TM,),
        in_specs=[pl.BlockSpec((PROJ_TM, d), lambda i: (i, 0)),
                  pl.BlockSpec((d, n), lambda i: (0, 0)),
                  pl.BlockSpec((d, LANES), lambda i: (0, 0)),
                  pl.BlockSpec((LANES, GLA_DK), lambda i: (0, 0)),
                  pl.BlockSpec((1, GLA_DK), lambda i: (0, 0))],
        out_specs=[pl.BlockSpec((PROJ_TM, n), lambda i: (i, 0)),
                   pl.BlockSpec((PROJ_TM, GLA_DK), lambda i: (i, 0))],
        out_shape=[jax.ShapeDtypeStruct((m, n), F32),
                   jax.ShapeDtypeStruct((m, GLA_DK), F32)],
        compiler_params=pltpu.CompilerParams(
            dimension_semantics=("parallel",), vmem_limit_bytes=VMEM_LIMIT),
        name="gla_proj",
    )(x, w_main, w_low, w_gk, b_gk.reshape(1, GLA_DK))


def _gla_kernel(q_ref, k_ref, v_ref, r_ref, lg_ref, cm_ref, ng_ref, o_ref, st_ref):
    @pl.when(pl.program_id(1) == 0)
    def _():
        st_ref[...] = jnp.zeros_like(st_ref)

    rb = GLA_RB
    lg = lg_ref[0]
    lg_hi = lg.astype(BF16)
    lg_lo = (lg - lg_hi.astype(F32)).astype(BF16)
    sums = jnp.dot(cm_ref[...], jnp.concatenate([lg_hi, lg_lo], axis=0),
                   preferred_element_type=F32)
    e_end = sums[:rb]
    dec_tot = jnp.exp(sums[rb:])
    q = q_ref[0] * (GLA_HEAD_K ** -0.5)
    k_dec = (k_ref[0] * jnp.exp(e_end)).astype(BF16)
    q_intra = q.astype(BF16)
    q_inter = (q * dec_tot).astype(BF16)
    v = v_ref[0].astype(BF16)
    r = r_ref[0]
    ng = ng_ref[...]

    row_chunk = lax.broadcasted_iota(jnp.int32, (rb, rb), 0) // CHUNK
    col_chunk = lax.broadcasted_iota(jnp.int32, (rb, rb), 1) // CHUNK
    same_chunk = row_chunk == col_chunk

    for h in range(GLA_HEADS):
        ks = slice(h * GLA_HEAD_K, (h + 1) * GLA_HEAD_K)
        vs = slice(h * GLA_HEAD_V, (h + 1) * GLA_HEAD_V)
        qh, qih, kh, vh = q_intra[:, ks], q_inter[:, ks], k_dec[:, ks], v[:, vs]
        scores = lax.dot_general(qh, kh, (((1,), (1,)), ((), ())), preferred_element_type=F32)
        scores = jnp.where(same_chunk, scores, 0.0).astype(BF16)
        o_intra = jnp.dot(scores, vh, preferred_element_type=F32)
        outs = []
        for c in range(rb // CHUNK):
            rows = slice(c * CHUNK, (c + 1) * CHUNK)
            st = st_ref[h]
            o_inter = lax.dot_general(qih[rows], st.astype(BF16), (((1,), (1,)), ((), ())),
                                      preferred_element_type=F32)
            outs.append(o_intra[rows] + o_inter)
            upd = lax.dot_general(vh[rows], kh[rows], (((0,), (0,)), ((), ())),
                                  preferred_element_type=F32)
            st_ref[h] = st * dec_tot[c * CHUNK:c * CHUNK + 1, ks] + upd
        o = jnp.concatenate(outs, axis=0)
        o = o * lax.rsqrt(jnp.mean(o * o, axis=-1, keepdims=True) + RMS_EPS) * ng
        o_ref[0, :, vs] = (_silu(r[:, vs]) * o).astype(BF16)


def _gla_chunk_matrix():
    idx = jnp.arange(GLA_RB)
    same = (idx[:, None] // CHUNK) == (idx[None, :] // CHUNK)
    later = same & (idx[None, :] > idx[:, None])
    top = jnp.concatenate([later, later], axis=1)
    bot = jnp.concatenate([same, same], axis=1)
    return jnp.concatenate([top, bot], axis=0).astype(BF16)


def _gla_core(proj, log_g, norm_g, bsz, seq):
    proj3 = proj.reshape(bsz, seq, proj.shape[1])
    lg3 = log_g.reshape(bsz, seq, GLA_DK)
    rb = GLA_RB
    return pl.pallas_call(
        _gla_kernel,
        grid=(bsz, seq // rb),
        in_specs=[
            pl.BlockSpec((1, rb, GLA_DK), lambda b, i: (b, i, 0)),
            pl.BlockSpec((1, rb, GLA_DK), lambda b, i: (b, i, 1)),
            pl.BlockSpec((1, rb, GLA_DV), lambda b, i: (b, i, 1)),
            pl.BlockSpec((1, rb, GLA_DV), lambda b, i: (b, i, 2)),
            pl.BlockSpec((1, rb, GLA_DK), lambda b, i: (b, i, 0)),
            pl.BlockSpec((2 * rb, 2 * rb), lambda b, i: (0, 0)),
            pl.BlockSpec((1, GLA_HEAD_V), lambda b, i: (0, 0)),
        ],
        out_specs=pl.BlockSpec((1, rb, GLA_DV), lambda b, i: (b, i, 0)),
        out_shape=jax.ShapeDtypeStruct((bsz, seq, GLA_DV), BF16),
        scratch_shapes=[pltpu.VMEM((GLA_HEADS, GLA_HEAD_V, GLA_HEAD_K), F32)],
        compiler_params=pltpu.CompilerParams(
            dimension_semantics=("parallel", "arbitrary"), vmem_limit_bytes=VMEM_LIMIT),
        name="gla_core",
    )(proj3, proj3, proj3, proj3, lg3, _gla_chunk_matrix(), norm_g.reshape(1, GLA_HEAD_V))


def _sb_kernel(q_ref, k_ref, v_ref, t_ref, o_ref):
    i = pl.program_id(2)
    qb, kb = SB_QB, SB_KB
    head0 = lax.broadcasted_iota(jnp.int32, (qb, LANES), 1) < SB_HEAD
    diag_mask = (lax.broadcasted_iota(jnp.int32, (2 * qb, kb), 1)
                 < lax.broadcasted_iota(jnp.int32, (2 * qb, kb), 0) % qb)
    tmat = t_ref[...]

    def split_heads(x):
        zero = jnp.zeros_like(x)
        return jnp.concatenate([jnp.where(head0, x, zero), jnp.where(head0, zero, x)], axis=0)

    def scores(q2, lanes, j):
        kblk = k_ref[0, pl.ds(pl.multiple_of(j * kb, kb), kb), lanes]
        return lax.dot_general(q2, kblk, (((1,), (1,)), ((), ())),
                               preferred_element_type=F32)

    def gates(z, causal):
        log_beta = jnp.minimum(z, 0.0) - jnp.log(1.0 + jnp.exp2(jnp.abs(z) * -LOG2E))
        log_keep = log_beta - z
        if causal is not None:
            log_keep = jnp.where(causal, log_keep, 0.0)
        lk_hi = log_keep.astype(BF16)
        lk_lo = (log_keep - lk_hi.astype(F32)).astype(BF16)
        return log_beta, jnp.concatenate([lk_hi, lk_lo], axis=1)

    def block_sums(lk):
        sums = jnp.dot(lk, tmat, preferred_element_type=F32)
        return sums[:, :kb], sums[:, kb:]

    def weights(log_beta, within, run, causal):
        a = jnp.exp(log_beta + within + run)
        if causal is not None:
            a = jnp.where(causal, a, 0.0)
        a = a.astype(BF16)
        return jnp.concatenate([a[:qb], a[qb:]], axis=1)

    def values(lanes, j):
        return split_heads(v_ref[0, pl.ds(pl.multiple_of(j * kb, kb), kb), lanes])

    def sweep(lanes, q2, j0, run, acc):
        def cond(carry):
            j, run, _ = carry
            return jnp.logical_and(j >= 0, jnp.max(run) > SB_LOG_ZERO)

        def body(carry):
            j, run, acc = carry
            log_beta, lk = gates(scores(q2, lanes, j), None)
            within, total = block_sums(lk)
            acc = acc + jnp.dot(weights(log_beta, within, run, None), values(lanes, j),
                                preferred_element_type=F32)
            return j - 1, run + total, acc

        return lax.while_loop(cond, body, (j0, run, acc))[2]

    pair_lanes = [slice(p * LANES, (p + 1) * LANES) for p in range(SB_PAIRS)]

    def window(nblk):
        first = i - (nblk - 1)
        units = [(p, w) for p in range(SB_PAIRS) for w in range(nblk)]
        mask = [diag_mask if w == nblk - 1 else None for w in range(nblk)]
        q2 = [split_heads(q_ref[0, :, lanes] * (SB_HEAD ** -0.5)) for lanes in pair_lanes]
        z = {(p, w): scores(q2[p], pair_lanes[p], first + w) for p, w in units}
        gate = {(p, w): gates(z[p, w], mask[w]) for p, w in units}
        sums = {(p, w): block_sums(gate[p, w][1]) for p, w in units}
        run = [jnp.zeros((2 * qb, kb), F32) for _ in pair_lanes]
        a_blocks = {}
        for p, w in reversed(units):
            a_blocks[p, w] = weights(gate[p, w][0], sums[p, w][0], run[p], mask[w])
            run[p] = run[p] + sums[p, w][1]
        left = functools.reduce(jnp.maximum, run)
        left = jnp.max(jnp.max(left, axis=0, keepdims=True)[:, :1])
        acc = [jnp.dot(jnp.concatenate([a_blocks[p, w] for w in range(nblk)], axis=1),
                       jnp.concatenate([values(lanes, first + w) for w in range(nblk)], axis=0),
                       preferred_element_type=F32) for p, lanes in enumerate(pair_lanes)]

        def rest():
            return tuple(sweep(lanes, q2[p], first - 1, run[p], acc[p])
                         for p, lanes in enumerate(pair_lanes))

        more = jnp.logical_and(first >= 1, left > SB_LOG_ZERO)
        outs = lax.cond(more, rest, lambda: tuple(acc))
        for p, lanes in enumerate(pair_lanes):
            o_ref[0, :, lanes] = outs[p].astype(o_ref.dtype)

    full = i >= SB_WINDOW - 1
    pl.when(full)(lambda: window(SB_WINDOW))
    pl.when(jnp.logical_not(full))(lambda: window(1))


def _sb_sum_matrix():
    j = jnp.arange(SB_KB)
    later = j[:, None] > j[None, :]
    half = jnp.concatenate([later, jnp.ones((SB_KB, SB_KB), bool)], axis=1)
    return jnp.concatenate([half, half], axis=0).astype(BF16)


def _sb_attn(q, kv, bsz, seq):
    q3 = q.reshape(bsz, seq, D_MODEL)
    kv3 = kv.reshape(bsz, seq, 2 * D_MODEL)
    width = SB_PAIRS * LANES
    groups = D_MODEL // width
    return pl.pallas_call(
        _sb_kernel,
        grid=(bsz, groups, seq // SB_QB),
        in_specs=[
            pl.BlockSpec((1, SB_QB, width), lambda b, p, i: (b, i, p)),
            pl.BlockSpec((1, seq, width), lambda b, p, i: (b, 0, p)),
            pl.BlockSpec((1, seq, width), lambda b, p, i: (b, 0, groups + p)),
            pl.BlockSpec((2 * SB_KB, 2 * SB_KB), lambda b, p, i: (0, 0)),
        ],
        out_specs=pl.BlockSpec((1, SB_QB, width), lambda b, p, i: (b, i, p)),
        out_shape=jax.ShapeDtypeStruct((bsz, seq, D_MODEL), BF16),
        compiler_params=pltpu.CompilerParams(
            dimension_semantics=("parallel", "parallel", "arbitrary"),
            vmem_limit_bytes=VMEM_LIMIT),
        name="sb_attn",
    )(q3, kv3, kv3, _sb_sum_matrix())


def kernel(x, ln_g, ln_b, ffn_w_up, ffn_w_down, gla_w_in, gla_w_gk, gla_b_gk, gla_norm_g,
           gla_w_out, sb_w_kv, sb_w_q, sb_w_out):
    bsz, seq, d = x.shape
    assert d == D_MODEL and seq % GLA_RB == 0 and seq % SB_QB == 0 and (bsz * seq) % FFN_TM == 0
    assert seq // SB_KB >= SB_WINDOW
    x = x.reshape(bsz * seq, d)
    w_up = ffn_w_up.astype(BF16)
    w_down = ffn_w_down.astype(BF16)

    kv = None
    for layer in range(DEPTH):
        if layer == N_A_LAYERS:
            kv = _matmul(x, sb_w_kv.astype(BF16), BF16)
        x = _ffn_ln(x, w_up[layer, 0], w_down[layer, 0], ln_g[layer, 0], ln_b[layer, 0])
        if layer < N_A_LAYERS:
            n_main = 2 * GLA_DK + 2 * GLA_DV
            w_in = gla_w_in[layer]
            w_low = jnp.pad(w_in[:, n_main:], ((0, 0), (0, LANES - GATE_RANK))).astype(BF16)
            w_gk = jnp.pad(gla_w_gk[layer], ((0, LANES - GATE_RANK), (0, 0))).astype(BF16)
            proj, log_g = _gla_proj(x, w_in[:, :n_main].astype(BF16), w_low, w_gk, gla_b_gk[layer])
            y = _gla_core(proj, log_g, gla_norm_g[layer], bsz, seq).reshape(bsz * seq, GLA_DV)
            w_out = gla_w_out[layer]
        else:
            j = layer - N_A_LAYERS
            q = _matmul(x, sb_w_q[j].astype(BF16), BF16)
            y = _sb_attn(q, kv, bsz, seq).reshape(bsz * seq, D_MODEL)
            w_out = sb_w_out[j]
        x = _proj_ln(y, x, w_out.astype(BF16), ln_g[layer, 1], ln_b[layer, 1])
        x = _ffn_ln(x, w_up[layer, 1], w_down[layer, 1], ln_g[layer, 2], ln_b[layer, 2])
    return x.reshape(bsz, seq, d)
```

```python
import functools

import jax
import jax.numpy as jnp
from jax import lax
from jax.experimental import pallas as pl
from jax.experimental.pallas import tpu as pltpu

F32 = jnp.float32
BF16 = jnp.bfloat16

D_MODEL = 1024
DEPTH = 2
CHUNK = 64
N_A_LAYERS = DEPTH // 2
D_FF = 2816
GLA_HEADS = 4
GLA_DK = D_MODEL // 2
GLA_DV = D_MODEL
GLA_HEAD_K = GLA_DK // GLA_HEADS
GLA_HEAD_V = GLA_DV // GLA_HEADS
GATE_RANK = 16
GATE_TAU = 16.0
SB_HEADS = 16
SB_HEAD = D_MODEL // SB_HEADS
DEEPNORM_ALPHA = (2 * DEPTH) ** 0.25
LN_EPS = 1e-5
RMS_EPS = 1e-6

LANES = 128
VMEM_LIMIT = 48 * 1024 * 1024

FFN_TM = 1024
FFN_TF = 256
FFN_LN_ROWS = 512
PROJ_TM = 512
GLA_RB = 256
SB_QB = 128
SB_KB = 128
SB_WINDOW = 3
SB_PAIRS = 4
SB_LOG_ZERO = -104.0
LOG2E = 1.4426950408889634


def _layer_norm(y, g, b):
    mu = jnp.mean(y, axis=-1, keepdims=True)
    yc = y - mu
    var = jnp.mean(yc * yc, axis=-1, keepdims=True)
    return yc * lax.rsqrt(var + LN_EPS) * g + b


def _silu(x):
    return x * jax.nn.sigmoid(x)


def _ffn_kernel(x_ref, wu_ref, wd_ref, g_ref, b_ref, o_ref, act_ref):
    xb = x_ref[...].astype(BF16)
    for c in range(D_FF // FFN_TF):
        gate = jnp.dot(xb, wu_ref[:, c * FFN_TF:(c + 1) * FFN_TF], preferred_element_type=F32)
        up = jnp.dot(xb, wu_ref[:, D_FF + c * FFN_TF:D_FF + (c + 1) * FFN_TF],
                     preferred_element_type=F32)
        act_ref[:, c * FFN_TF:(c + 1) * FFN_TF] = (_silu(gate) * up).astype(BF16)
    for r in range(FFN_TM // FFN_LN_ROWS):
        rows = slice(r * FFN_LN_ROWS, (r + 1) * FFN_LN_ROWS)
        down = jnp.dot(act_ref[rows, :], wd_ref[...], preferred_element_type=F32)
        y = DEEPNORM_ALPHA * x_ref[rows, :] + 0.5 * down
        o_ref[rows, :] = _layer_norm(y, g_ref[...], b_ref[...])


def _ffn_ln(x, w_up, w_down, layer, half, g, b):
    m, d = x.shape
    resident = pl.Buffered(1)
    return pl.pallas_call(
        _ffn_kernel,
        grid=(m // FFN_TM,),
        in_specs=[
            pl.BlockSpec((FFN_TM, d), lambda i: (i, 0)),
            pl.BlockSpec((None, None, d, 2 * D_FF), lambda i: (layer, half, 0, 0),
                         pipeline_mode=resident),
            pl.BlockSpec((None, None, D_FF, d), lambda i: (layer, half, 0, 0),
                         pipeline_mode=resident),
            pl.BlockSpec((1, d), lambda i: (0, 0)),
            pl.BlockSpec((1, d), lambda i: (0, 0)),
        ],
        out_specs=pl.BlockSpec((FFN_TM, d), lambda i: (i, 0)),
        out_shape=jax.ShapeDtypeStruct((m, d), F32),
        scratch_shapes=[pltpu.VMEM((FFN_TM, D_FF), BF16)],
        compiler_params=pltpu.CompilerParams(
            dimension_semantics=("parallel",), vmem_limit_bytes=VMEM_LIMIT),
        name="ffn_ln",
    )(x, w_up, w_down, g.reshape(1, d), b.reshape(1, d))


def _matmul_kernel(x_ref, w_ref, o_ref):
    o_ref[...] = jnp.dot(x_ref[...].astype(BF16), w_ref[...],
                         preferred_element_type=F32).astype(o_ref.dtype)


def _matmul(x, w, out_dtype):
    m, d = x.shape
    n = w.shape[1]
    return pl.pallas_call(
        _matmul_kernel,
        grid=(m // PROJ_TM,),
        in_specs=[pl.BlockSpec((PROJ_TM, d), lambda i: (i, 0)),
                  pl.BlockSpec((d, n), lambda i: (0, 0))],
        out_specs=pl.BlockSpec((PROJ_TM, n), lambda i: (i, 0)),
        out_shape=jax.ShapeDtypeStruct((m, n), out_dtype),
        compiler_params=pltpu.CompilerParams(
            dimension_semantics=("parallel",), vmem_limit_bytes=VMEM_LIMIT),
        name="matmul",
    )(x, w)


def _proj_ln_kernel(y_ref, x_ref, w_ref, g_ref, b_ref, o_ref):
    mix = jnp.dot(y_ref[...], w_ref[...], preferred_element_type=F32)
    o_ref[...] = _layer_norm(DEEPNORM_ALPHA * x_ref[...] + mix, g_ref[...], b_ref[...])


def _proj_ln(y, x, w, g, b):
    m, d = x.shape
    return pl.pallas_call(
        _proj_ln_kernel,
        grid=(m // PROJ_TM,),
        in_specs=[pl.BlockSpec((PROJ_TM, y.shape[1]), lambda i: (i, 0)),
                  pl.BlockSpec((PROJ_TM, d), lambda i: (i, 0)),
                  pl.BlockSpec(w.shape, lambda i: (0, 0)),
                  pl.BlockSpec((1, d), lambda i: (0, 0)),
                  pl.BlockSpec((1, d), lambda i: (0, 0))],
        out_specs=pl.BlockSpec((PROJ_TM, d), lambda i: (i, 0)),
        out_shape=jax.ShapeDtypeStruct((m, d), F32),
        compiler_params=pltpu.CompilerParams(
            dimension_semantics=("parallel",), vmem_limit_bytes=VMEM_LIMIT),
        name="proj_ln",
    )(y, x, w, g.reshape(1, d), b.reshape(1, d))


def _gla_proj_kernel(x_ref, wm_ref, wl_ref, wgk_ref, bgk_ref, p_ref, lg_ref):
    xb = x_ref[...].astype(BF16)
    p_ref[...] = jnp.dot(xb, wm_ref[...], preferred_element_type=F32)
    low = jnp.dot(xb, wl_ref[...], preferred_element_type=F32)
    pre = jnp.dot(low.astype(BF16), wgk_ref[...], preferred_element_type=F32) + bgk_ref[...]
    lg_ref[...] = jax.nn.log_sigmoid(pre) / GATE_TAU


def _gla_proj(x, w_main, w_low, w_gk, b_gk):
    m, d = x.shape
    n = w_main.shape[1]
    return pl.pallas_call(
        _gla_proj_kernel,
        grid=(m // PROJ_TM,),
        in_specs=[pl.BlockSpec((PROJ_TM, d), lambda i: (i, 0)),
                  pl.BlockSpec((d, n), lambda i: (0, 0)),
                  pl.BlockSpec((d, LANES), lambda i: (0, 0)),
                  pl.BlockSpec((LANES, GLA_DK), lambda i: (0, 0)),
                  pl.BlockSpec((1, GLA_DK), lambda i: (0, 0))],
        out_specs=[pl.BlockSpec((PROJ_TM, n), lambda i: (i, 0)),
                   pl.BlockSpec((PROJ_TM, GLA_DK), lambda i: (i, 0))],
        out_shape=[jax.ShapeDtypeStruct((m, n), F32),
                   jax.ShapeDtypeStruct((m, GLA_DK), F32)],
        compiler_params=pltpu.CompilerParams(
            dimension_semantics=("parallel",), vmem_limit_bytes=VMEM_LIMIT),
        name="gla_proj",
    )(x, w_main, w_low, w_gk, b_gk.reshape(1, GLA_DK))


def _gla_kernel(q_ref, k_ref, v_ref, r_ref, lg_ref, cm_ref, ng_ref, o_ref, st_ref):
    @pl.when(pl.program_id(1) == 0)
    def _():
        st_ref[...] = jnp.zeros_like(st_ref)

    rb = GLA_RB
    lg = lg_ref[0]
    lg_hi = lg.astype(BF16)
    lg_lo = (lg - lg_hi.astype(F32)).astype(BF16)
    sums = jnp.dot(cm_ref[...], jnp.concatenate([lg_hi, lg_lo], axis=0),
                   preferred_element_type=F32)
    e_end = sums[:rb]
    dec_tot = jnp.exp(sums[rb:])
    q = q_ref[0] * (GLA_HEAD_K ** -0.5)
    k_dec = (k_ref[0] * jnp.exp(e_end)).astype(BF16)
    q_intra = q.astype(BF16)
    q_inter = (q * dec_tot).astype(BF16)
    v = v_ref[0].astype(BF16)
    r = r_ref[0]
    ng = ng_ref[...]

    row_chunk = lax.broadcasted_iota(jnp.int32, (rb, rb), 0) // CHUNK
    col_chunk = lax.broadcasted_iota(jnp.int32, (rb, rb), 1) // CHUNK
    same_chunk = row_chunk == col_chunk

    for h in range(GLA_HEADS):
        ks = slice(h * GLA_HEAD_K, (h + 1) * GLA_HEAD_K)
        vs = slice(h * GLA_HEAD_V, (h + 1) * GLA_HEAD_V)
        qh, qih, kh, vh = q_intra[:, ks], q_inter[:, ks], k_dec[:, ks], v[:, vs]
        scores = lax.dot_general(qh, kh, (((1,), (1,)), ((), ())), preferred_element_type=F32)
        scores = jnp.where(same_chunk, scores, 0.0).astype(BF16)
        o_intra = jnp.dot(scores, vh, preferred_element_type=F32)
        outs = []
        for c in range(rb // CHUNK):
            rows = slice(c * CHUNK, (c + 1) * CHUNK)
            st = st_ref[h]
            o_inter = lax.dot_general(qih[rows], st.astype(BF16), (((1,), (1,)), ((), ())),
                                      preferred_element_type=F32)
            outs.append(o_intra[rows] + o_inter)
            upd = lax.dot_general(vh[rows], kh[rows], (((0,), (0,)), ((), ())),
                                  preferred_element_type=F32)
            st_ref[h] = st * dec_tot[c * CHUNK:c * CHUNK + 1, ks] + upd
        o = jnp.concatenate(outs, axis=0)
        o = o * lax.rsqrt(jnp.mean(o * o, axis=-1, keepdims=True) + RMS_EPS) * ng
        o_ref[0, :, vs] = (_silu(r[:, vs]) * o).astype(BF16)


def _gla_chunk_matrix():
    idx = jnp.arange(GLA_RB)
    same = (idx[:, None] // CHUNK) == (idx[None, :] // CHUNK)
    later = same & (idx[None, :] > idx[:, None])
    top = jnp.concatenate([later, later], axis=1)
    bot = jnp.concatenate([same, same], axis=1)
    return jnp.concatenate([top, bot], axis=0).astype(BF16)


def _gla_core(proj, log_g, norm_g, bsz, seq):
    proj3 = proj.reshape(bsz, seq, proj.shape[1])
    lg3 = log_g.reshape(bsz, seq, GLA_DK)
    rb = GLA_RB
    return pl.pallas_call(
        _gla_kernel,
        grid=(bsz, seq // rb),
        in_specs=[
            pl.BlockSpec((1, rb, GLA_DK), lambda b, i: (b, i, 0)),
            pl.BlockSpec((1, rb, GLA_DK), lambda b, i: (b, i, 1)),
            pl.BlockSpec((1, rb, GLA_DV), lambda b, i: (b, i, 1)),
            pl.BlockSpec((1, rb, GLA_DV), lambda b, i: (b, i, 2)),
            pl.BlockSpec((1, rb, GLA_DK), lambda b, i: (b, i, 0)),
            pl.BlockSpec((2 * rb, 2 * rb), lambda b, i: (0, 0)),
            pl.BlockSpec((1, GLA_HEAD_V), lambda b, i: (0, 0)),
        ],
        out_specs=pl.BlockSpec((1, rb, GLA_DV), lambda b, i: (b, i, 0)),
        out_shape=jax.ShapeDtypeStruct((bsz, seq, GLA_DV), BF16),
        scratch_shapes=[pltpu.VMEM((GLA_HEADS, GLA_HEAD_V, GLA_HEAD_K), F32)],
        compiler_params=pltpu.CompilerParams(
            dimension_semantics=("parallel", "arbitrary"), vmem_limit_bytes=VMEM_LIMIT),
        name="gla_core",
    )(proj3, proj3, proj3, proj3, lg3, _gla_chunk_matrix(), norm_g.reshape(1, GLA_HEAD_V))


def _sb_kernel(q_ref, k_ref, v_ref, t_ref, o_ref):
    i = pl.program_id(2)
    qb, kb = SB_QB, SB_KB
    head0 = lax.broadcasted_iota(jnp.int32, (qb, LANES), 1) < SB_HEAD
    diag_mask = (lax.broadcasted_iota(jnp.int32, (2 * qb, kb), 1)
                 < lax.broadcasted_iota(jnp.int32, (2 * qb, kb), 0) % qb)
    tmat = t_ref[...]

    def split_heads(x):
        zero = jnp.zeros_like(x)
        return jnp.concatenate([jnp.where(head0, x, zero), jnp.where(head0, zero, x)], axis=0)

    def scores(q2, lanes, j):
        kblk = k_ref[0, pl.ds(pl.multiple_of(j * kb, kb), kb), lanes]
        return lax.dot_general(q2, kblk, (((1,), (1,)), ((), ())),
                               preferred_element_type=F32)

    def gates(z, causal):
        log_beta = jnp.minimum(z, 0.0) - jnp.log(1.0 + jnp.exp2(jnp.abs(z) * -LOG2E))
        log_keep = log_beta - z
        if causal is not None:
            log_keep = jnp.where(causal, log_keep, 0.0)
        lk_hi = log_keep.astype(BF16)
        lk_lo = (log_keep - lk_hi.astype(F32)).astype(BF16)
        return log_beta, jnp.concatenate([lk_hi, lk_lo], axis=1)

    def block_sums(lk):
        sums = jnp.dot(lk, tmat, preferred_element_type=F32)
        return sums[:, :kb], sums[:, kb:]

    def weights(log_beta, within, run, causal):
        a = jnp.exp(log_beta + within + run)
        if causal is not None:
            a = jnp.where(causal, a, 0.0)
        a = a.astype(BF16)
        return jnp.concatenate([a[:qb], a[qb:]], axis=1)

    def values(lanes, j):
        return split_heads(v_ref[0, pl.ds(pl.multiple_of(j * kb, kb), kb), lanes])

    def sweep(lanes, q2, j0, run, acc):
        def cond(carry):
            j, run, _ = carry
            return jnp.logical_and(j >= 0, jnp.max(run) > SB_LOG_ZERO)

        def body(carry):
            j, run, acc = carry
            log_beta, lk = gates(scores(q2, lanes, j), None)
            within, total = block_sums(lk)
            acc = acc + jnp.dot(weights(log_beta, within, run, None), values(lanes, j),
                                preferred_element_type=F32)
            return j - 1, run + total, acc

        return lax.while_loop(cond, body, (j0, run, acc))[2]

    pair_lanes = [slice(p * LANES, (p + 1) * LANES) for p in range(SB_PAIRS)]

    def window(nblk):
        first = i - (nblk - 1)
        units = [(p, w) for p in range(SB_PAIRS) for w in range(nblk)]
        mask = [diag_mask if w == nblk - 1 else None for w in range(nblk)]
        q2 = [split_heads(q_ref[0, :, lanes] * (SB_HEAD ** -0.5)) for lanes in pair_lanes]
        z = {(p, w): scores(q2[p], pair_lanes[p], first + w) for p, w in units}
        gate = {(p, w): gates(z[p, w], mask[w]) for p, w in units}
        sums = {(p, w): block_sums(gate[p, w][1]) for p, w in units}
        run = [jnp.zeros((2 * qb, kb), F32) for _ in pair_lanes]
        a_blocks = {}
        for p, w in reversed(units):
            a_blocks[p, w] = weights(gate[p, w][0], sums[p, w][0], run[p], mask[w])
            run[p] = run[p] + sums[p, w][1]
        left = functools.reduce(jnp.maximum, run)
        left = jnp.max(jnp.max(left, axis=0, keepdims=True)[:, :1])
        acc = [jnp.dot(jnp.concatenate([a_blocks[p, w] for w in range(nblk)], axis=1),
                       jnp.concatenate([values(lanes, first + w) for w in range(nblk)], axis=0),
                       preferred_element_type=F32) for p, lanes in enumerate(pair_lanes)]

        def rest():
            return tuple(sweep(lanes, q2[p], first - 1, run[p], acc[p])
                         for p, lanes in enumerate(pair_lanes))

        more = jnp.logical_and(first >= 1, left > SB_LOG_ZERO)
        outs = lax.cond(more, rest, lambda: tuple(acc))
        for p, lanes in enumerate(pair_lanes):
            o_ref[0, :, lanes] = outs[p].astype(o_ref.dtype)

    full = i >= SB_WINDOW - 1
    pl.when(full)(lambda: window(SB_WINDOW))
    pl.when(jnp.logical_not(full))(lambda: window(1))


def _sb_sum_matrix():
    j = jnp.arange(SB_KB)
    later = j[:, None] > j[None, :]
    half = jnp.concatenate([later, jnp.ones((SB_KB, SB_KB), bool)], axis=1)
    return jnp.concatenate([half, half], axis=0).astype(BF16)


def _sb_attn(q, kv, bsz, seq):
    q3 = q.reshape(bsz, seq, D_MODEL)
    kv3 = kv.reshape(bsz, seq, 2 * D_MODEL)
    width = SB_PAIRS * LANES
    groups = D_MODEL // width
    return pl.pallas_call(
        _sb_kernel,
        grid=(bsz, groups, seq // SB_QB),
        in_specs=[
            pl.BlockSpec((1, SB_QB, width), lambda b, p, i: (b, i, p)),
            pl.BlockSpec((1, seq, width), lambda b, p, i: (b, 0, p)),
            pl.BlockSpec((1, seq, width), lambda b, p, i: (b, 0, groups + p)),
            pl.BlockSpec((2 * SB_KB, 2 * SB_KB), lambda b, p, i: (0, 0)),
        ],
        out_specs=pl.BlockSpec((1, SB_QB, width), lambda b, p, i: (b, i, p)),
        out_shape=jax.ShapeDtypeStruct((bsz, seq, D_MODEL), BF16),
        compiler_params=pltpu.CompilerParams(
            dimension_semantics=("parallel", "parallel", "arbitrary"),
            vmem_limit_bytes=VMEM_LIMIT),
        name="sb_attn",
    )(q3, kv3, kv3, _sb_sum_matrix())


def kernel(x, ln_g, ln_b, ffn_w_up, ffn_w_down, gla_w_in, gla_w_gk, gla_b_gk, gla_norm_g,
           gla_w_out, sb_w_kv, sb_w_q, sb_w_out):
    bsz, seq, d = x.shape
    assert d == D_MODEL and seq % GLA_RB == 0 and seq % SB_QB == 0 and (bsz * seq) % FFN_TM == 0
    assert seq // SB_KB >= SB_WINDOW
    x = x.reshape(bsz * seq, d)
    w_up = ffn_w_up.astype(BF16)
    w_down = ffn_w_down.astype(BF16)

    kv = None
    for layer in range(DEPTH):
        if layer == N_A_LAYERS:
            kv = _matmul(x, sb_w_kv.astype(BF16), BF16)
        x = _ffn_ln(x, w_up, w_down, layer, 0, ln_g[layer, 0], ln_b[layer, 0])
        if layer < N_A_LAYERS:
            n_main = 2 * GLA_DK + 2 * GLA_DV
            w_in = gla_w_in[layer]
            w_low = jnp.pad(w_in[:, n_main:], ((0, 0), (0, LANES - GATE_RANK))).astype(BF16)
            w_gk = jnp.pad(gla_w_gk[layer], ((0, LANES - GATE_RANK), (0, 0))).astype(BF16)
            proj, log_g = _gla_proj(x, w_in[:, :n_main].astype(BF16), w_low, w_gk, gla_b_gk[layer])
            y = _gla_core(proj, log_g, gla_norm_g[layer], bsz, seq).reshape(bsz * seq, GLA_DV)
            w_out = gla_w_out[layer]
        else:
            j = layer - N_A_LAYERS
            q = _matmul(x, sb_w_q[j].astype(BF16), BF16)
            y = _sb_attn(q, kv, bsz, seq).reshape(bsz * seq, D_MODEL)
            w_out = sb_w_out[j]
        x = _proj_ln(y, x, w_out.astype(BF16), ln_g[layer, 1], ln_b[layer, 1])
        x = _ffn_ln(x, w_up, w_down, layer, 1, ln_g[layer, 2], ln_b[layer, 2])
    return x.reshape(bsz, seq, d)
```

```python
import functools

import jax
import jax.numpy as jnp
from jax import lax
from jax.experimental import pallas as pl
from jax.experimental.pallas import tpu as pltpu

F32 = jnp.float32
BF16 = jnp.bfloat16

D_MODEL = 1024
DEPTH = 2
CHUNK = 64
N_A_LAYERS = DEPTH // 2
D_FF = 2816
GLA_HEADS = 4
GLA_DK = D_MODEL // 2
GLA_DV = D_MODEL
GLA_HEAD_K = GLA_DK // GLA_HEADS
GLA_HEAD_V = GLA_DV // GLA_HEADS
GATE_RANK = 16
GATE_TAU = 16.0
SB_HEADS = 16
SB_HEAD = D_MODEL // SB_HEADS
DEEPNORM_ALPHA = (2 * DEPTH) ** 0.25
LN_EPS = 1e-5
RMS_EPS = 1e-6

LANES = 128
VMEM_LIMIT = 48 * 1024 * 1024
VMEM_TEMP_BYTES = 10 * 1024 * 1024

BLK_TM = 512
BLK_LN_ROWS = 256
FFN_TF = 256
GLA_RB = 256
SB_QB = 128
SB_KB = 128
SB_WINDOW = 3
SB_PAIRS = 4
SB_LOG_ZERO = -104.0
LOG2E = 1.4426950408889634


def _layer_norm(y, g, b):
    mu = jnp.mean(y, axis=-1, keepdims=True)
    yc = y - mu
    var = jnp.mean(yc * yc, axis=-1, keepdims=True)
    return yc * lax.rsqrt(var + LN_EPS) * g + b


def _silu(x):
    return x * jax.nn.sigmoid(x)


def _block_kernel(*refs, pre, post):
    refs = list(refs)

    def take(n):
        out, refs[:] = refs[:n], refs[n:]
        return out

    if pre:
        y_ref, xres_ref, wo_ref, g1_ref, b1_ref = take(5)
    else:
        (x_ref,) = take(1)
    wu_ref, wd_ref, g2_ref, b2_ref = take(4)
    if post == "gla":
        wm_ref, wl_ref, wgk_ref, bgk_ref = take(4)
    elif post == "matmul":
        (wp_ref,) = take(1)
    (o_ref,) = take(1)
    if post == "gla":
        p_ref, lg_ref = take(2)
    elif post == "matmul":
        (p_ref,) = take(1)
    (act_ref,) = take(1)
    if pre:
        (x_ref,) = take(1)
    if post:
        (ob_ref,) = take(1)
    assert not refs

    row_blocks = [slice(r * BLK_LN_ROWS, (r + 1) * BLK_LN_ROWS)
                  for r in range(BLK_TM // BLK_LN_ROWS)]
    if pre:
        for rows in row_blocks:
            mix = jnp.dot(y_ref[rows, :], wo_ref[...], preferred_element_type=F32)
            x_ref[rows, :] = _layer_norm(DEEPNORM_ALPHA * xres_ref[rows, :] + mix,
                                         g1_ref[...], b1_ref[...])

    xb = x_ref[...].astype(BF16)
    for c in range(D_FF // FFN_TF):
        gate = jnp.dot(xb, wu_ref[:, c * FFN_TF:(c + 1) * FFN_TF], preferred_element_type=F32)
        up = jnp.dot(xb, wu_ref[:, D_FF + c * FFN_TF:D_FF + (c + 1) * FFN_TF],
                     preferred_element_type=F32)
        act_ref[:, c * FFN_TF:(c + 1) * FFN_TF] = (_silu(gate) * up).astype(BF16)

    for rows in row_blocks:
        down = jnp.dot(act_ref[rows, :], wd_ref[...], preferred_element_type=F32)
        out = _layer_norm(DEEPNORM_ALPHA * x_ref[rows, :] + 0.5 * down, g2_ref[...], b2_ref[...])
        o_ref[rows, :] = out
        if post:
            ob_ref[rows, :] = out.astype(BF16)

    if post == "matmul":
        for rows in row_blocks:
            p_ref[rows, :] = jnp.dot(ob_ref[rows, :], wp_ref[...],
                                     preferred_element_type=F32).astype(p_ref.dtype)
    elif post == "gla":
        for rows in row_blocks:
            ob = ob_ref[rows, :]
            p_ref[rows, :] = jnp.dot(ob, wm_ref[...], preferred_element_type=F32)
            low = jnp.dot(ob, wl_ref[...], preferred_element_type=F32)
            gate_pre = jnp.dot(low.astype(BF16), wgk_ref[...],
                               preferred_element_type=F32) + bgk_ref[...]
            lg_ref[rows, :] = jax.nn.log_sigmoid(gate_pre) / GATE_TAU


def _nbytes(shape, dtype):
    n = jnp.dtype(dtype).itemsize
    for s in shape:
        n *= s
    return n


def _block(w_up, w_down, layer, half, g2, b2, *, x=None, pre=None, post=None):
    d = D_MODEL
    m = (pre[1] if pre else x).shape[0]
    resident = pl.Buffered(1)
    vmem = [0]

    def tile(n, dtype):
        vmem[0] += 2 * _nbytes((BLK_TM, n), dtype)
        return pl.BlockSpec((BLK_TM, n), lambda i: (i, 0))

    def whole(a):
        vmem[0] += _nbytes(a.shape, a.dtype)
        return pl.BlockSpec(a.shape, lambda i: (0,) * a.ndim, pipeline_mode=resident)

    def stacked(a):
        vmem[0] += _nbytes(a.shape[2:], a.dtype)
        return pl.BlockSpec((None, None) + a.shape[2:], lambda i: (layer, half, 0, 0),
                            pipeline_mode=resident)

    def scratch(n, dtype):
        vmem[0] += _nbytes((BLK_TM, n), dtype)
        return pltpu.VMEM((BLK_TM, n), dtype)

    args, in_specs = [], []
    if pre:
        y, xres, w_out, g1, b1 = pre
        args += [y, xres, w_out, g1.reshape(1, d), b1.reshape(1, d)]
        in_specs += [tile(y.shape[1], y.dtype), tile(d, F32)] + [whole(a) for a in args[2:]]
    else:
        args += [x]
        in_specs += [tile(d, F32)]
    vecs = [g2.reshape(1, d), b2.reshape(1, d)]
    args += [w_up, w_down] + vecs
    in_specs += [stacked(w_up), stacked(w_down)] + [whole(a) for a in vecs]
    out_shape = [jax.ShapeDtypeStruct((m, d), F32)]
    out_specs = [tile(d, F32)]
    kind = post[0] if post else None
    if kind == "matmul":
        w_p = post[1]
        args += [w_p]
        in_specs += [whole(w_p)]
        out_shape += [jax.ShapeDtypeStruct((m, w_p.shape[1]), BF16)]
        out_specs += [tile(w_p.shape[1], BF16)]
    elif kind == "gla":
        w_main, w_low, w_gk, b_gk = post[1:]
        extra = [w_main, w_low, w_gk, b_gk.reshape(1, GLA_DK)]
        args += extra
        in_specs += [whole(a) for a in extra]
        out_shape += [jax.ShapeDtypeStruct((m, w_main.shape[1]), F32),
                      jax.ShapeDtypeStruct((m, GLA_DK), F32)]
        out_specs += [tile(w_main.shape[1], F32), tile(GLA_DK, F32)]
    scratch_shapes = [scratch(D_FF, BF16)]
    if pre:
        scratch_shapes += [scratch(d, F32)]
    if post:
        scratch_shapes += [scratch(d, BF16)]
    return pl.pallas_call(
        functools.partial(_block_kernel, pre=bool(pre), post=kind),
        grid=(m // BLK_TM,),
        in_specs=in_specs,
        out_specs=out_specs,
        out_shape=out_shape,
        scratch_shapes=scratch_shapes,
        compiler_params=pltpu.CompilerParams(
            dimension_semantics=("parallel",), vmem_limit_bytes=vmem[0] + VMEM_TEMP_BYTES),
        name="block" + ("_pre" if pre else "") + ("_" + kind if kind else ""),
    )(*args)


def _gla_kernel(q_ref, k_ref, v_ref, r_ref, lg_ref, cm_ref, ng_ref, o_ref, st_ref):
    @pl.when(pl.program_id(1) == 0)
    def _():
        st_ref[...] = jnp.zeros_like(st_ref)

    rb = GLA_RB
    lg = lg_ref[0]
    lg_hi = lg.astype(BF16)
    lg_lo = (lg - lg_hi.astype(F32)).astype(BF16)
    sums = jnp.dot(cm_ref[...], jnp.concatenate([lg_hi, lg_lo], axis=0),
                   preferred_element_type=F32)
    e_end = sums[:rb]
    dec_tot = jnp.exp(sums[rb:])
    q = q_ref[0] * (GLA_HEAD_K ** -0.5)
    k_dec = (k_ref[0] * jnp.exp(e_end)).astype(BF16)
    q_intra = q.astype(BF16)
    q_inter = (q * dec_tot).astype(BF16)
    v = v_ref[0].astype(BF16)
    r = r_ref[0]
    ng = ng_ref[...]

    row_chunk = lax.broadcasted_iota(jnp.int32, (rb, rb), 0) // CHUNK
    col_chunk = lax.broadcasted_iota(jnp.int32, (rb, rb), 1) // CHUNK
    same_chunk = row_chunk == col_chunk

    for h in range(GLA_HEADS):
        ks = slice(h * GLA_HEAD_K, (h + 1) * GLA_HEAD_K)
        vs = slice(h * GLA_HEAD_V, (h + 1) * GLA_HEAD_V)
        qh, qih, kh, vh = q_intra[:, ks], q_inter[:, ks], k_dec[:, ks], v[:, vs]
        scores = lax.dot_general(qh, kh, (((1,), (1,)), ((), ())), preferred_element_type=F32)
        scores = jnp.where(same_chunk, scores, 0.0).astype(BF16)
        o_intra = jnp.dot(scores, vh, preferred_element_type=F32)
        outs = []
        for c in range(rb // CHUNK):
            rows = slice(c * CHUNK, (c + 1) * CHUNK)
            st = st_ref[h]
            o_inter = lax.dot_general(qih[rows], st.astype(BF16), (((1,), (1,)), ((), ())),
                                      preferred_element_type=F32)
            outs.append(o_intra[rows] + o_inter)
            upd = lax.dot_general(vh[rows], kh[rows], (((0,), (0,)), ((), ())),
                                  preferred_element_type=F32)
            st_ref[h] = st * dec_tot[c * CHUNK:c * CHUNK + 1, ks] + upd
        o = jnp.concatenate(outs, axis=0)
        o = o * lax.rsqrt(jnp.mean(o * o, axis=-1, keepdims=True) + RMS_EPS) * ng
        o_ref[0, :, vs] = (_silu(r[:, vs]) * o).astype(BF16)


def _gla_chunk_matrix():
    idx = jnp.arange(GLA_RB)
    same = (idx[:, None] // CHUNK) == (idx[None, :] // CHUNK)
    later = same & (idx[None, :] > idx[:, None])
    top = jnp.concatenate([later, later], axis=1)
    bot = jnp.concatenate([same, same], axis=1)
    return jnp.concatenate([top, bot], axis=0).astype(BF16)


def _gla_core(proj, log_g, norm_g, bsz, seq):
    proj3 = proj.reshape(bsz, seq, proj.shape[1])
    lg3 = log_g.reshape(bsz, seq, GLA_DK)
    rb = GLA_RB
    return pl.pallas_call(
        _gla_kernel,
        grid=(bsz, seq // rb),
        in_specs=[
            pl.BlockSpec((1, rb, GLA_DK), lambda b, i: (b, i, 0)),
            pl.BlockSpec((1, rb, GLA_DK), lambda b, i: (b, i, 1)),
            pl.BlockSpec((1, rb, GLA_DV), lambda b, i: (b, i, 1)),
            pl.BlockSpec((1, rb, GLA_DV), lambda b, i: (b, i, 2)),
            pl.BlockSpec((1, rb, GLA_DK), lambda b, i: (b, i, 0)),
            pl.BlockSpec((2 * rb, 2 * rb), lambda b, i: (0, 0)),
            pl.BlockSpec((1, GLA_HEAD_V), lambda b, i: (0, 0)),
        ],
        out_specs=pl.BlockSpec((1, rb, GLA_DV), lambda b, i: (b, i, 0)),
        out_shape=jax.ShapeDtypeStruct((bsz, seq, GLA_DV), BF16),
        scratch_shapes=[pltpu.VMEM((GLA_HEADS, GLA_HEAD_V, GLA_HEAD_K), F32)],
        compiler_params=pltpu.CompilerParams(
            dimension_semantics=("parallel", "arbitrary"), vmem_limit_bytes=VMEM_LIMIT),
        name="gla_core",
    )(proj3, proj3, proj3, proj3, lg3, _gla_chunk_matrix(), norm_g.reshape(1, GLA_HEAD_V))


def _sb_kernel(q_ref, k_ref, v_ref, t_ref, o_ref):
    i = pl.program_id(2)
    qb, kb = SB_QB, SB_KB
    head0 = lax.broadcasted_iota(jnp.int32, (qb, LANES), 1) < SB_HEAD
    diag_mask = (lax.broadcasted_iota(jnp.int32, (2 * qb, kb), 1)
                 < lax.broadcasted_iota(jnp.int32, (2 * qb, kb), 0) % qb)
    tmat = t_ref[...]

    def split_heads(x):
        zero = jnp.zeros_like(x)
        return jnp.concatenate([jnp.where(head0, x, zero), jnp.where(head0, zero, x)], axis=0)

    def scores(q2, lanes, j):
        kblk = k_ref[0, pl.ds(pl.multiple_of(j * kb, kb), kb), lanes]
        return lax.dot_general(q2, kblk, (((1,), (1,)), ((), ())),
                               preferred_element_type=F32)

    def gates(z, causal):
        log_beta = jnp.minimum(z, 0.0) - jnp.log(1.0 + jnp.exp2(jnp.abs(z) * -LOG2E))
        log_keep = log_beta - z
        if causal is not None:
            log_keep = jnp.where(causal, log_keep, 0.0)
        lk_hi = log_keep.astype(BF16)
        lk_lo = (log_keep - lk_hi.astype(F32)).astype(BF16)
        return log_beta, jnp.concatenate([lk_hi, lk_lo], axis=1)

    def block_sums(lk):
        sums = jnp.dot(lk, tmat, preferred_element_type=F32)
        return sums[:, :kb], sums[:, kb:]

    def weights(log_beta, within, run, causal):
        a = jnp.exp(log_beta + within + run)
        if causal is not None:
            a = jnp.where(causal, a, 0.0)
        a = a.astype(BF16)
        return jnp.concatenate([a[:qb], a[qb:]], axis=1)

    def values(lanes, j):
        return split_heads(v_ref[0, pl.ds(pl.multiple_of(j * kb, kb), kb), lanes])

    def sweep(lanes, q2, j0, run, acc):
        def cond(carry):
            j, run, _ = carry
            return jnp.logical_and(j >= 0, jnp.max(run) > SB_LOG_ZERO)

        def body(carry):
            j, run, acc = carry
            log_beta, lk = gates(scores(q2, lanes, j), None)
            within, total = block_sums(lk)
            acc = acc + jnp.dot(weights(log_beta, within, run, None), values(lanes, j),
                                preferred_element_type=F32)
            return j - 1, run + total, acc

        return lax.while_loop(cond, body, (j0, run, acc))[2]

    pair_lanes = [slice(p * LANES, (p + 1) * LANES) for p in range(SB_PAIRS)]

    def window(nblk):
        first = i - (nblk - 1)
        units = [(p, w) for p in range(SB_PAIRS) for w in range(nblk)]
        mask = [diag_mask if w == nblk - 1 else None for w in range(nblk)]
        q2 = [split_heads(q_ref[0, :, lanes] * (SB_HEAD ** -0.5)) for lanes in pair_lanes]
        z = {(p, w): scores(q2[p], pair_lanes[p], first + w) for p, w in units}
        gate = {(p, w): gates(z[p, w], mask[w]) for p, w in units}
        sums = {(p, w): block_sums(gate[p, w][1]) for p, w in units}
        run = [jnp.zeros((2 * qb, kb), F32) for _ in pair_lanes]
        a_blocks = {}
        for p, w in reversed(units):
            a_blocks[p, w] = weights(gate[p, w][0], sums[p, w][0], run[p], mask[w])
            run[p] = run[p] + sums[p, w][1]
        left = functools.reduce(jnp.maximum, run)
        left = jnp.max(jnp.max(left, axis=0, keepdims=True)[:, :1])
        acc = [jnp.dot(jnp.concatenate([a_blocks[p, w] for w in range(nblk)], axis=1),
                       jnp.concatenate([values(lanes, first + w) for w in range(nblk)], axis=0),
                       preferred_element_type=F32) for p, lanes in enumerate(pair_lanes)]

        def rest():
            return tuple(sweep(lanes, q2[p], first - 1, run[p], acc[p])
                         for p, lanes in enumerate(pair_lanes))

        more = jnp.logical_and(first >= 1, left > SB_LOG_ZERO)
        outs = lax.cond(more, rest, lambda: tuple(acc))
        for p, lanes in enumerate(pair_lanes):
            o_ref[0, :, lanes] = outs[p].astype(o_ref.dtype)

    full = i >= SB_WINDOW - 1
    pl.when(full)(lambda: window(SB_WINDOW))
    pl.when(jnp.logical_not(full))(lambda: window(1))


def _sb_sum_matrix():
    j = jnp.arange(SB_KB)
    later = j[:, None] > j[None, :]
    half = jnp.concatenate([later, jnp.ones((SB_KB, SB_KB), bool)], axis=1)
    return jnp.concatenate([half, half], axis=0).astype(BF16)


def _sb_attn(q, kv, bsz, seq):
    q3 = q.reshape(bsz, seq, D_MODEL)
    kv3 = kv.reshape(bsz, seq, 2 * D_MODEL)
    width = SB_PAIRS * LANES
    groups = D_MODEL // width
    return pl.pallas_call(
        _sb_kernel,
        grid=(bsz, groups, seq // SB_QB),
        in_specs=[
            pl.BlockSpec((1, SB_QB, width), lambda b, p, i: (b, i, p)),
            pl.BlockSpec((1, seq, width), lambda b, p, i: (b, 0, p)),
            pl.BlockSpec((1, seq, width), lambda b, p, i: (b, 0, groups + p)),
            pl.BlockSpec((2 * SB_KB, 2 * SB_KB), lambda b, p, i: (0, 0)),
        ],
        out_specs=pl.BlockSpec((1, SB_QB, width), lambda b, p, i: (b, i, p)),
        out_shape=jax.ShapeDtypeStruct((bsz, seq, D_MODEL), BF16),
        compiler_params=pltpu.CompilerParams(
            dimension_semantics=("parallel", "parallel", "arbitrary"),
            vmem_limit_bytes=VMEM_LIMIT),
        name="sb_attn",
    )(q3, kv3, kv3, _sb_sum_matrix())


def kernel(x, ln_g, ln_b, ffn_w_up, ffn_w_down, gla_w_in, gla_w_gk, gla_b_gk, gla_norm_g,
           gla_w_out, sb_w_kv, sb_w_q, sb_w_out):
    bsz, seq, d = x.shape
    assert d == D_MODEL and seq % GLA_RB == 0 and seq % SB_QB == 0 and (bsz * seq) % BLK_TM == 0
    assert seq // SB_KB >= SB_WINDOW and DEPTH == 2 and N_A_LAYERS == 1
    x = x.reshape(bsz * seq, d)
    w_up = ffn_w_up.astype(BF16)
    w_down = ffn_w_down.astype(BF16)

    n_main = 2 * GLA_DK + 2 * GLA_DV
    w_in = gla_w_in[0]
    w_low = jnp.pad(w_in[:, n_main:], ((0, 0), (0, LANES - GATE_RANK))).astype(BF16)
    w_gk = jnp.pad(gla_w_gk[0], ((0, LANES - GATE_RANK), (0, 0))).astype(BF16)
    x, proj, log_g = _block(w_up, w_down, 0, 0, ln_g[0, 0], ln_b[0, 0], x=x,
                            post=("gla", w_in[:, :n_main].astype(BF16), w_low, w_gk, gla_b_gk[0]))
    y = _gla_core(proj, log_g, gla_norm_g[0], bsz, seq).reshape(bsz * seq, GLA_DV)
    x, kv = _block(w_up, w_down, 0, 1, ln_g[0, 2], ln_b[0, 2],
                   pre=(y, x, gla_w_out[0].astype(BF16), ln_g[0, 1], ln_b[0, 1]),
                   post=("matmul", sb_w_kv.astype(BF16)))

    x, q = _block(w_up, w_down, 1, 0, ln_g[1, 0], ln_b[1, 0], x=x,
                  post=("matmul", sb_w_q[0].astype(BF16)))
    y = _sb_attn(q, kv, bsz, seq).reshape(bsz * seq, D_MODEL)
    (x,) = _block(w_up, w_down, 1, 1, ln_g[1, 2], ln_b[1, 2],
                  pre=(y, x, sb_w_out[0].astype(BF16), ln_g[1, 1], ln_b[1, 1]))
    return x.reshape(bsz, seq, d)
```

```python
import functools

import jax
import jax.numpy as jnp
from jax import lax
from jax.experimental import pallas as pl
from jax.experimental.pallas import tpu as pltpu

F32 = jnp.float32
BF16 = jnp.bfloat16

D_MODEL = 1024
DEPTH = 2
CHUNK = 64
N_A_LAYERS = DEPTH // 2
D_FF = 2816
GLA_HEADS = 4
GLA_DK = D_MODEL // 2
GLA_DV = D_MODEL
GLA_HEAD_K = GLA_DK // GLA_HEADS
GLA_HEAD_V = GLA_DV // GLA_HEADS
GATE_RANK = 16
GATE_TAU = 16.0
SB_HEADS = 16
SB_HEAD = D_MODEL // SB_HEADS
DEEPNORM_ALPHA = (2 * DEPTH) ** 0.25
LN_EPS = 1e-5
RMS_EPS = 1e-6

LANES = 128
VMEM_LIMIT = 48 * 1024 * 1024
VMEM_TEMP_BYTES = 10 * 1024 * 1024

BLK_TM = 512
BLK_LN_ROWS = 256
FFN_TF = 256
GLA_RB = 256
SB_QB = 128
SB_KB = 128
SB_SLAB = 64
SB_SLAB_BLOCKS = 2
SB_PAIRS = 8
SB_LOG_ZERO = -104.0
LOG2E = 1.4426950408889634


def _layer_norm(y, g, b):
    mu = jnp.mean(y, axis=-1, keepdims=True)
    yc = y - mu
    var = jnp.mean(yc * yc, axis=-1, keepdims=True)
    return yc * lax.rsqrt(var + LN_EPS) * g + b


def _silu(x):
    return x * jax.nn.sigmoid(x)


def _block_kernel(*refs, pre, post):
    refs = list(refs)

    def take(n):
        out, refs[:] = refs[:n], refs[n:]
        return out

    if pre:
        y_ref, xres_ref, wo_ref, g1_ref, b1_ref = take(5)
    else:
        (x_ref,) = take(1)
    wu_ref, wd_ref, g2_ref, b2_ref = take(4)
    if post == "gla":
        wm_ref, wl_ref, wgk_ref, bgk_ref = take(4)
    elif post == "matmul":
        (wp_ref,) = take(1)
    (o_ref,) = take(1)
    if post == "gla":
        p_ref, lg_ref = take(2)
    elif post == "matmul":
        (p_ref,) = take(1)
    (act_ref,) = take(1)
    if pre:
        (x_ref,) = take(1)
    if post:
        (ob_ref,) = take(1)
    assert not refs

    row_blocks = [slice(r * BLK_LN_ROWS, (r + 1) * BLK_LN_ROWS)
                  for r in range(BLK_TM // BLK_LN_ROWS)]
    if pre:
        for rows in row_blocks:
            mix = jnp.dot(y_ref[rows, :], wo_ref[...], preferred_element_type=F32)
            x_ref[rows, :] = _layer_norm(DEEPNORM_ALPHA * xres_ref[rows, :] + mix,
                                         g1_ref[...], b1_ref[...])

    xb = x_ref[...].astype(BF16)
    for c in range(D_FF // FFN_TF):
        gate = jnp.dot(xb, wu_ref[:, c * FFN_TF:(c + 1) * FFN_TF], preferred_element_type=F32)
        up = jnp.dot(xb, wu_ref[:, D_FF + c * FFN_TF:D_FF + (c + 1) * FFN_TF],
                     preferred_element_type=F32)
        act_ref[:, c * FFN_TF:(c + 1) * FFN_TF] = (_silu(gate) * up).astype(BF16)

    for rows in row_blocks:
        down = jnp.dot(act_ref[rows, :], wd_ref[...], preferred_element_type=F32)
        out = _layer_norm(DEEPNORM_ALPHA * x_ref[rows, :] + 0.5 * down, g2_ref[...], b2_ref[...])
        o_ref[rows, :] = out
        if post:
            ob_ref[rows, :] = out.astype(BF16)

    if post == "matmul":
        for rows in row_blocks:
            p_ref[rows, :] = jnp.dot(ob_ref[rows, :], wp_ref[...],
                                     preferred_element_type=F32).astype(p_ref.dtype)
    elif post == "gla":
        for rows in row_blocks:
            ob = ob_ref[rows, :]
            p_ref[rows, :] = jnp.dot(ob, wm_ref[...], preferred_element_type=F32)
            low = jnp.dot(ob, wl_ref[...], preferred_element_type=F32)
            gate_pre = jnp.dot(low.astype(BF16), wgk_ref[...],
                               preferred_element_type=F32) + bgk_ref[...]
            lg_ref[rows, :] = jax.nn.log_sigmoid(gate_pre) / GATE_TAU


def _nbytes(shape, dtype):
    n = jnp.dtype(dtype).itemsize
    for s in shape:
        n *= s
    return n


def _block(w_up, w_down, layer, half, g2, b2, *, x=None, pre=None, post=None):
    d = D_MODEL
    m = (pre[1] if pre else x).shape[0]
    resident = pl.Buffered(1)
    vmem = [0]

    def tile(n, dtype):
        vmem[0] += 2 * _nbytes((BLK_TM, n), dtype)
        return pl.BlockSpec((BLK_TM, n), lambda i: (i, 0))

    def whole(a):
        vmem[0] += _nbytes(a.shape, a.dtype)
        return pl.BlockSpec(a.shape, lambda i: (0,) * a.ndim, pipeline_mode=resident)

    def stacked(a):
        vmem[0] += _nbytes(a.shape[2:], a.dtype)
        return pl.BlockSpec((None, None) + a.shape[2:], lambda i: (layer, half, 0, 0),
                            pipeline_mode=resident)

    def scratch(n, dtype):
        vmem[0] += _nbytes((BLK_TM, n), dtype)
        return pltpu.VMEM((BLK_TM, n), dtype)

    args, in_specs = [], []
    if pre:
        y, xres, w_out, g1, b1 = pre
        args += [y, xres, w_out, g1.reshape(1, d), b1.reshape(1, d)]
        in_specs += [tile(y.shape[1], y.dtype), tile(d, F32)] + [whole(a) for a in args[2:]]
    else:
        args += [x]
        in_specs += [tile(d, F32)]
    vecs = [g2.reshape(1, d), b2.reshape(1, d)]
    args += [w_up, w_down] + vecs
    in_specs += [stacked(w_up), stacked(w_down)] + [whole(a) for a in vecs]
    out_shape = [jax.ShapeDtypeStruct((m, d), F32)]
    out_specs = [tile(d, F32)]
    kind = post[0] if post else None
    if kind == "matmul":
        w_p = post[1]
        args += [w_p]
        in_specs += [whole(w_p)]
        out_shape += [jax.ShapeDtypeStruct((m, w_p.shape[1]), BF16)]
        out_specs += [tile(w_p.shape[1], BF16)]
    elif kind == "gla":
        w_main, w_low, w_gk, b_gk = post[1:]
        extra = [w_main, w_low, w_gk, b_gk.reshape(1, GLA_DK)]
        args += extra
        in_specs += [whole(a) for a in extra]
        out_shape += [jax.ShapeDtypeStruct((m, w_main.shape[1]), F32),
                      jax.ShapeDtypeStruct((m, GLA_DK), F32)]
        out_specs += [tile(w_main.shape[1], F32), tile(GLA_DK, F32)]
    scratch_shapes = [scratch(D_FF, BF16)]
    if pre:
        scratch_shapes += [scratch(d, F32)]
    if post:
        scratch_shapes += [scratch(d, BF16)]
    return pl.pallas_call(
        functools.partial(_block_kernel, pre=bool(pre), post=kind),
        grid=(m // BLK_TM,),
        in_specs=in_specs,
        out_specs=out_specs,
        out_shape=out_shape,
        scratch_shapes=scratch_shapes,
        compiler_params=pltpu.CompilerParams(
            dimension_semantics=("parallel",), vmem_limit_bytes=vmem[0] + VMEM_TEMP_BYTES),
        name="block" + ("_pre" if pre else "") + ("_" + kind if kind else ""),
    )(*args)


def _gla_kernel(q_ref, k_ref, v_ref, r_ref, lg_ref, cm_ref, ng_ref, o_ref, st_ref):
    @pl.when(pl.program_id(1) == 0)
    def _():
        st_ref[...] = jnp.zeros_like(st_ref)

    rb = GLA_RB
    lg = lg_ref[0]
    lg_hi = lg.astype(BF16)
    lg_lo = (lg - lg_hi.astype(F32)).astype(BF16)
    sums = jnp.dot(cm_ref[...], jnp.concatenate([lg_hi, lg_lo], axis=0),
                   preferred_element_type=F32)
    e_end = sums[:rb]
    dec_tot = jnp.exp(sums[rb:])
    q = q_ref[0] * (GLA_HEAD_K ** -0.5)
    k_dec = (k_ref[0] * jnp.exp(e_end)).astype(BF16)
    q_intra = q.astype(BF16)
    q_inter = (q * dec_tot).astype(BF16)
    v = v_ref[0].astype(BF16)
    r = r_ref[0]
    ng = ng_ref[...]

    row_chunk = lax.broadcasted_iota(jnp.int32, (rb, rb), 0) // CHUNK
    col_chunk = lax.broadcasted_iota(jnp.int32, (rb, rb), 1) // CHUNK
    same_chunk = row_chunk == col_chunk

    for h in range(GLA_HEADS):
        ks = slice(h * GLA_HEAD_K, (h + 1) * GLA_HEAD_K)
        vs = slice(h * GLA_HEAD_V, (h + 1) * GLA_HEAD_V)
        qh, qih, kh, vh = q_intra[:, ks], q_inter[:, ks], k_dec[:, ks], v[:, vs]
        scores = lax.dot_general(qh, kh, (((1,), (1,)), ((), ())), preferred_element_type=F32)
        scores = jnp.where(same_chunk, scores, 0.0).astype(BF16)
        o_intra = jnp.dot(scores, vh, preferred_element_type=F32)
        outs = []
        for c in range(rb // CHUNK):
            rows = slice(c * CHUNK, (c + 1) * CHUNK)
            st = st_ref[h]
            o_inter = lax.dot_general(qih[rows], st.astype(BF16), (((1,), (1,)), ((), ())),
                                      preferred_element_type=F32)
            outs.append(o_intra[rows] + o_inter)
            upd = lax.dot_general(vh[rows], kh[rows], (((0,), (0,)), ((), ())),
                                  preferred_element_type=F32)
            st_ref[h] = st * dec_tot[c * CHUNK:c * CHUNK + 1, ks] + upd
        o = jnp.concatenate(outs, axis=0)
        o = o * lax.rsqrt(jnp.mean(o * o, axis=-1, keepdims=True) + RMS_EPS) * ng
        o_ref[0, :, vs] = (_silu(r[:, vs]) * o).astype(BF16)


def _gla_chunk_matrix():
    idx = jnp.arange(GLA_RB)
    same = (idx[:, None] // CHUNK) == (idx[None, :] // CHUNK)
    later = same & (idx[None, :] > idx[:, None])
    top = jnp.concatenate([later, later], axis=1)
    bot = jnp.concatenate([same, same], axis=1)
    return jnp.concatenate([top, bot], axis=0).astype(BF16)


def _gla_core(proj, log_g, norm_g, bsz, seq):
    proj3 = proj.reshape(bsz, seq, proj.shape[1])
    lg3 = log_g.reshape(bsz, seq, GLA_DK)
    rb = GLA_RB
    return pl.pallas_call(
        _gla_kernel,
        grid=(bsz, seq // rb),
        in_specs=[
            pl.BlockSpec((1, rb, GLA_DK), lambda b, i: (b, i, 0)),
            pl.BlockSpec((1, rb, GLA_DK), lambda b, i: (b, i, 1)),
            pl.BlockSpec((1, rb, GLA_DV), lambda b, i: (b, i, 1)),
            pl.BlockSpec((1, rb, GLA_DV), lambda b, i: (b, i, 2)),
            pl.BlockSpec((1, rb, GLA_DK), lambda b, i: (b, i, 0)),
            pl.BlockSpec((2 * rb, 2 * rb), lambda b, i: (0, 0)),
            pl.BlockSpec((1, GLA_HEAD_V), lambda b, i: (0, 0)),
        ],
        out_specs=pl.BlockSpec((1, rb, GLA_DV), lambda b, i: (b, i, 0)),
        out_shape=jax.ShapeDtypeStruct((bsz, seq, GLA_DV), BF16),
        scratch_shapes=[pltpu.VMEM((GLA_HEADS, GLA_HEAD_V, GLA_HEAD_K), F32)],
        compiler_params=pltpu.CompilerParams(
            dimension_semantics=("parallel", "arbitrary"), vmem_limit_bytes=VMEM_LIMIT),
        name="gla_core",
    )(proj3, proj3, proj3, proj3, lg3, _gla_chunk_matrix(), norm_g.reshape(1, GLA_HEAD_V))


def _sb_kernel(q_ref, k_ref, v_ref, t_ref, o_ref):
    i = pl.program_id(2)
    qb, kb = SB_QB, SB_KB
    tmat = t_ref[...]
    pair_lanes = [slice(p * LANES, (p + 1) * LANES) for p in range(SB_PAIRS)]

    def split_heads(x):
        zero = jnp.zeros_like(x)
        keep = lax.broadcasted_iota(jnp.int32, x.shape, 1) < SB_HEAD
        return jnp.concatenate([jnp.where(keep, x, zero), jnp.where(keep, zero, x)], axis=0)

    def queries(lanes, row0, rows):
        return split_heads(q_ref[0, row0:row0 + rows, lanes] * (SB_HEAD ** -0.5))

    def scores(q2, lanes, start, nkeys):
        kwin = k_ref[0, pl.ds(start, nkeys), lanes]
        return lax.dot_general(q2, kwin, (((1,), (1,)), ((), ())),
                               preferred_element_type=F32)

    def gates(z, visible):
        log_beta = jnp.minimum(z, 0.0) - jnp.log(1.0 + jnp.exp2(jnp.abs(z) * -LOG2E))
        log_keep = log_beta - z
        if visible is not None:
            log_keep = jnp.where(visible, log_keep, 0.0)
        lk_hi = log_keep.astype(BF16)
        lk_lo = (log_keep - lk_hi.astype(F32)).astype(BF16)
        return log_beta, jnp.concatenate([lk_hi, lk_lo], axis=1)

    def block_sums(lk):
        sums = jnp.dot(lk, tmat, preferred_element_type=F32)
        return sums[:, :kb], sums[:, kb:]

    def weights(log_beta, within, run, visible):
        a = jnp.exp(log_beta + within + run)
        if visible is not None:
            a = jnp.where(visible, a, 0.0)
        return a.astype(BF16)

    def attend(a_blocks, lanes, start):
        rows = a_blocks[0].shape[0] // 2
        vwin = v_ref[0, pl.ds(start, len(a_blocks) * kb), lanes]
        per_head = [jnp.dot(jnp.concatenate([a[hd * rows:(hd + 1) * rows] for a in a_blocks], axis=1),
                            vwin, preferred_element_type=F32) for hd in range(2)]
        keep = lax.broadcasted_iota(jnp.int32, (rows, LANES), 1) < SB_HEAD
        return jnp.where(keep, per_head[0], per_head[1])

    def sweep(lanes, j0, covered, run, acc):
        q2 = queries(lanes, 0, qb)
        s_loc = lax.broadcasted_iota(jnp.int32, (2 * qb, kb), 1)

        def cond(carry):
            j, run, _ = carry
            return jnp.logical_and(j >= 0, jnp.max(run) > SB_LOG_ZERO)

        def body(carry):
            j, run, acc = carry
            start = pl.multiple_of(j * kb, kb)
            fresh = start + s_loc < covered
            log_beta, lk = gates(scores(q2, lanes, start, kb), fresh)
            within, total = block_sums(lk)
            acc = acc + attend([weights(log_beta, within, run, fresh)], lanes, start)
            return j - 1, run + total, acc

        return lax.while_loop(cond, body, (j0, run, acc))[2]

    def window(rows, nblk):
        nslab = qb // rows
        units = [(p, h) for p in range(SB_PAIRS) for h in range(nslab)]
        win0 = [pl.multiple_of(i * qb + (h + 1) * rows - nblk * kb, rows) for h in range(nslab)]
        r_loc = lax.broadcasted_iota(jnp.int32, (2 * rows, kb), 0) % rows
        s_loc = lax.broadcasted_iota(jnp.int32, (2 * rows, kb), 1)
        last_visible = s_loc < r_loc + (kb - rows)
        mask = [last_visible if w == nblk - 1 else None for w in range(nblk)]

        q2 = {(p, h): queries(pair_lanes[p], h * rows, rows) for p, h in units}
        z = {u: scores(q2[u], pair_lanes[u[0]], win0[u[1]], nblk * kb) for u in units}
        gate = {(u, w): gates(z[u][:, w * kb:(w + 1) * kb], mask[w])
                for u in units for w in range(nblk)}
        sums = {uw: block_sums(gate[uw][1]) for uw in gate}
        run, acc = {}, {}
        for u in units:
            run[u] = jnp.zeros((2 * rows, kb), F32)
            a_blocks = [None] * nblk
            for w in reversed(range(nblk)):
                a_blocks[w] = weights(gate[u, w][0], sums[u, w][0], run[u], mask[w])
                run[u] = run[u] + sums[u, w][1]
            acc[u] = attend(a_blocks, pair_lanes[u[0]], win0[u[1]])
        left = functools.reduce(jnp.maximum, run.values())
        left = jnp.max(jnp.max(left, axis=0, keepdims=True)[:, :1])

        def done():
            return tuple(jnp.concatenate([acc[p, h] for h in range(nslab)], axis=0)
                         for p in range(SB_PAIRS))

        def rest():
            covered = jnp.concatenate([jnp.full((rows, kb), 1, jnp.int32) * win0[h]
                                       for _ in range(2) for h in range(nslab)], axis=0)
            outs = []
            for p, lanes in enumerate(pair_lanes):
                run_p = jnp.concatenate([run[p, h][hd * rows:(hd + 1) * rows]
                                         for hd in range(2) for h in range(nslab)], axis=0)
                outs.append(sweep(lanes, j0, covered, run_p, done()[p]))
            return tuple(outs)

        j0 = (win0[-1] + kb - 1) // kb - 1
        more = jnp.logical_and(j0 >= 0, left > SB_LOG_ZERO)
        outs = lax.cond(more, rest, done)
        for p, lanes in enumerate(pair_lanes):
            o_ref[0, :, lanes] = outs[p].astype(o_ref.dtype)

    full = i * qb + SB_SLAB >= SB_SLAB_BLOCKS * kb
    pl.when(full)(lambda: window(SB_SLAB, SB_SLAB_BLOCKS))
    pl.when(jnp.logical_not(full))(lambda: window(qb, 1))


def _sb_sum_matrix():
    j = jnp.arange(SB_KB)
    later = j[:, None] > j[None, :]
    half = jnp.concatenate([later, jnp.ones((SB_KB, SB_KB), bool)], axis=1)
    return jnp.concatenate([half, half], axis=0).astype(BF16)


def _sb_attn(q, kv, bsz, seq):
    q3 = q.reshape(bsz, seq, D_MODEL)
    kv3 = kv.reshape(bsz, seq, 2 * D_MODEL)
    width = SB_PAIRS * LANES
    groups = D_MODEL // width
    return pl.pallas_call(
        _sb_kernel,
        grid=(bsz, groups, seq // SB_QB),
        in_specs=[
            pl.BlockSpec((1, SB_QB, width), lambda b, p, i: (b, i, p)),
            pl.BlockSpec((1, seq, width), lambda b, p, i: (b, 0, p)),
            pl.BlockSpec((1, seq, width), lambda b, p, i: (b, 0, groups + p)),
            pl.BlockSpec((2 * SB_KB, 2 * SB_KB), lambda b, p, i: (0, 0)),
        ],
        out_specs=pl.BlockSpec((1, SB_QB, width), lambda b, p, i: (b, i, p)),
        out_shape=jax.ShapeDtypeStruct((bsz, seq, D_MODEL), BF16),
        compiler_params=pltpu.CompilerParams(
            dimension_semantics=("parallel", "parallel", "arbitrary"),
            vmem_limit_bytes=VMEM_LIMIT),
        name="sb_attn",
    )(q3, kv3, kv3, _sb_sum_matrix())


def kernel(x, ln_g, ln_b, ffn_w_up, ffn_w_down, gla_w_in, gla_w_gk, gla_b_gk, gla_norm_g,
           gla_w_out, sb_w_kv, sb_w_q, sb_w_out):
    bsz, seq, d = x.shape
    assert d == D_MODEL and seq % GLA_RB == 0 and seq % SB_QB == 0 and (bsz * seq) % BLK_TM == 0
    assert SB_QB == SB_KB and SB_QB % SB_SLAB == 0 and DEPTH == 2 and N_A_LAYERS == 1
    x = x.reshape(bsz * seq, d)
    w_up = ffn_w_up.astype(BF16)
    w_down = ffn_w_down.astype(BF16)

    n_main = 2 * GLA_DK + 2 * GLA_DV
    w_in = gla_w_in[0]
    w_low = jnp.pad(w_in[:, n_main:], ((0, 0), (0, LANES - GATE_RANK))).astype(BF16)
    w_gk = jnp.pad(gla_w_gk[0], ((0, LANES - GATE_RANK), (0, 0))).astype(BF16)
    x, proj, log_g = _block(w_up, w_down, 0, 0, ln_g[0, 0], ln_b[0, 0], x=x,
                            post=("gla", w_in[:, :n_main].astype(BF16), w_low, w_gk, gla_b_gk[0]))
    y = _gla_core(proj, log_g, gla_norm_g[0], bsz, seq).reshape(bsz * seq, GLA_DV)
    x, kv = _block(w_up, w_down, 0, 1, ln_g[0, 2], ln_b[0, 2],
                   pre=(y, x, gla_w_out[0].astype(BF16), ln_g[0, 1], ln_b[0, 1]),
                   post=("matmul", sb_w_kv.astype(BF16)))

    x, q = _block(w_up, w_down, 1, 0, ln_g[1, 0], ln_b[1, 0], x=x,
                  post=("matmul", sb_w_q[0].astype(BF16)))
    y = _sb_attn(q, kv, bsz, seq).reshape(bsz * seq, D_MODEL)
    (x,) = _block(w_up, w_down, 1, 1, ln_g[1, 2], ln_b[1, 2],
                  pre=(y, x, sb_w_out[0].astype(BF16), ln_g[1, 1], ln_b[1, 1]))
    return x.reshape(bsz, seq, d)
```

```python
import functools

import jax
import jax.numpy as jnp
from jax import lax
from jax.experimental import pallas as pl
from jax.experimental.pallas import tpu as pltpu

F32 = jnp.float32
BF16 = jnp.bfloat16

D_MODEL = 1024
DEPTH = 2
CHUNK = 64
N_A_LAYERS = DEPTH // 2
D_FF = 2816
GLA_HEADS = 4
GLA_DK = D_MODEL // 2
GLA_DV = D_MODEL
GLA_HEAD_K = GLA_DK // GLA_HEADS
GLA_HEAD_V = GLA_DV // GLA_HEADS
GATE_RANK = 16
GATE_TAU = 16.0
SB_HEADS = 16
SB_HEAD = D_MODEL // SB_HEADS
DEEPNORM_ALPHA = (2 * DEPTH) ** 0.25
LN_EPS = 1e-5
RMS_EPS = 1e-6

LANES = 128
VMEM_LIMIT = 48 * 1024 * 1024
VMEM_TEMP_BYTES = 10 * 1024 * 1024

BLK_TM = 512
BLK_LN_ROWS = 256
FFN_TF = 256
GLA_RB = 256
GLA_STEP_BLOCKS = 4
SB_QB = 128
SB_KB = 128
SB_SLAB = 64
SB_SLAB_BLOCKS = 2
SB_PAIRS = 8
SB_LOG_ZERO = -104.0
SB_MASKED_LOGIT = -1e30
LOG2E = 1.4426950408889634


def _layer_norm(y, g, b):
    mu = jnp.mean(y, axis=-1, keepdims=True)
    yc = y - mu
    var = jnp.mean(yc * yc, axis=-1, keepdims=True)
    return yc * lax.rsqrt(var + LN_EPS) * g + b


def _silu(x):
    return x * jax.nn.sigmoid(x)


def _block_kernel(*refs, pre, post):
    refs = list(refs)

    def take(n):
        out, refs[:] = refs[:n], refs[n:]
        return out

    if pre:
        y_ref, xres_ref, wo_ref, g1_ref, b1_ref = take(5)
    else:
        (x_ref,) = take(1)
    wu_ref, wd_ref, g2_ref, b2_ref = take(4)
    if post == "gla":
        wm_ref, wl_ref, wgk_ref, bgk_ref = take(4)
    elif post == "matmul":
        (wp_ref,) = take(1)
    (o_ref,) = take(1)
    if post == "gla":
        p_ref, lg_ref = take(2)
    elif post == "matmul":
        (p_ref,) = take(1)
    (act_ref,) = take(1)
    if pre:
        (x_ref,) = take(1)
    if post:
        (ob_ref,) = take(1)
    assert not refs

    row_blocks = [slice(r * BLK_LN_ROWS, (r + 1) * BLK_LN_ROWS)
                  for r in range(BLK_TM // BLK_LN_ROWS)]
    if pre:
        for rows in row_blocks:
            mix = jnp.dot(y_ref[rows, :], wo_ref[...], preferred_element_type=F32)
            x_ref[rows, :] = _layer_norm(DEEPNORM_ALPHA * xres_ref[rows, :] + mix,
                                         g1_ref[...], b1_ref[...])

    for rows in (row_blocks if pre else [slice(0, BLK_TM)]):
        xb = x_ref[rows, :].astype(BF16)
        for c in range(D_FF // FFN_TF):
            cols = slice(c * FFN_TF, (c + 1) * FFN_TF)
            gate = jnp.dot(xb, wu_ref[:, cols], preferred_element_type=F32)
            up = jnp.dot(xb, wu_ref[:, D_FF + c * FFN_TF:D_FF + (c + 1) * FFN_TF],
                         preferred_element_type=F32)
            act_ref[rows, cols] = (_silu(gate) * up).astype(BF16)

    for rows in row_blocks:
        down = jnp.dot(act_ref[rows, :], wd_ref[...], preferred_element_type=F32)
        out = _layer_norm(DEEPNORM_ALPHA * x_ref[rows, :] + 0.5 * down, g2_ref[...], b2_ref[...])
        o_ref[rows, :] = out
        if post:
            ob_ref[rows, :] = out.astype(BF16)

    if post == "matmul":
        for rows in row_blocks:
            p_ref[rows, :] = jnp.dot(ob_ref[rows, :], wp_ref[...],
                                     preferred_element_type=F32).astype(p_ref.dtype)
    elif post == "gla":
        for rows in row_blocks:
            ob = ob_ref[rows, :]
            p_ref[rows, :] = jnp.dot(ob, wm_ref[...], preferred_element_type=F32)
            low = jnp.dot(ob, wl_ref[...], preferred_element_type=F32)
            gate_pre = jnp.dot(low.astype(BF16), wgk_ref[...],
                               preferred_element_type=F32) + bgk_ref[...]
            lg_ref[rows, :] = jax.nn.log_sigmoid(gate_pre) / GATE_TAU


def _nbytes(shape, dtype):
    n = jnp.dtype(dtype).itemsize
    for s in shape:
        n *= s
    return n


def _block(w_up, w_down, layer, half, g2, b2, *, x=None, pre=None, post=None):
    d = D_MODEL
    m = (pre[1] if pre else x).shape[0]
    resident = pl.Buffered(1)
    vmem = [0]

    def tile(n, dtype):
        vmem[0] += 2 * _nbytes((BLK_TM, n), dtype)
        return pl.BlockSpec((BLK_TM, n), lambda i: (i, 0))

    def whole(a):
        vmem[0] += _nbytes(a.shape, a.dtype)
        return pl.BlockSpec(a.shape, lambda i: (0,) * a.ndim, pipeline_mode=resident)

    def stacked(a):
        vmem[0] += _nbytes(a.shape[2:], a.dtype)
        return pl.BlockSpec((None, None) + a.shape[2:], lambda i: (layer, half, 0, 0),
                            pipeline_mode=resident)

    def scratch(n, dtype):
        vmem[0] += _nbytes((BLK_TM, n), dtype)
        return pltpu.VMEM((BLK_TM, n), dtype)

    args, in_specs = [], []
    if pre:
        y, xres, w_out, g1, b1 = pre
        args += [y, xres, w_out, g1.reshape(1, d), b1.reshape(1, d)]
        in_specs += [tile(y.shape[1], y.dtype), tile(d, F32)] + [whole(a) for a in args[2:]]
    else:
        args += [x]
        in_specs += [tile(d, F32)]
    vecs = [g2.reshape(1, d), b2.reshape(1, d)]
    args += [w_up, w_down] + vecs
    in_specs += [stacked(w_up), stacked(w_down)] + [whole(a) for a in vecs]
    out_shape = [jax.ShapeDtypeStruct((m, d), F32)]
    out_specs = [tile(d, F32)]
    kind = post[0] if post else None
    if kind == "matmul":
        w_p = post[1]
        args += [w_p]
        in_specs += [whole(w_p)]
        out_shape += [jax.ShapeDtypeStruct((m, w_p.shape[1]), BF16)]
        out_specs += [tile(w_p.shape[1], BF16)]
    elif kind == "gla":
        w_main, w_low, w_gk, b_gk = post[1:]
        extra = [w_main, w_low, w_gk, b_gk.reshape(1, GLA_DK)]
        args += extra
        in_specs += [whole(a) for a in extra]
        out_shape += [jax.ShapeDtypeStruct((m, w_main.shape[1]), F32),
                      jax.ShapeDtypeStruct((m, GLA_DK), F32)]
        out_specs += [tile(w_main.shape[1], F32), tile(GLA_DK, F32)]
    scratch_shapes = [scratch(D_FF, BF16)]
    if pre:
        scratch_shapes += [scratch(d, F32)]
    if post:
        scratch_shapes += [scratch(d, BF16)]
    return pl.pallas_call(
        functools.partial(_block_kernel, pre=bool(pre), post=kind),
        grid=(m // BLK_TM,),
        in_specs=in_specs,
        out_specs=out_specs,
        out_shape=out_shape,
        scratch_shapes=scratch_shapes,
        compiler_params=pltpu.CompilerParams(
            dimension_semantics=("parallel",), vmem_limit_bytes=vmem[0] + VMEM_TEMP_BYTES),
        name="block" + ("_pre" if pre else "") + ("_" + kind if kind else ""),
    )(*args)


def _gla_kernel(q_ref, k_ref, v_ref, r_ref, lg_ref, cm_ref, ng_ref, o_ref, st_ref):
    @pl.when(pl.program_id(1) == 0)
    def _():
        st_ref[...] = jnp.zeros_like(st_ref)

    rb = GLA_RB
    heads = range(GLA_HEADS)
    chunks = range(rb // CHUNK)
    ks = [slice(h * GLA_HEAD_K, (h + 1) * GLA_HEAD_K) for h in heads]
    vs = [slice(h * GLA_HEAD_V, (h + 1) * GLA_HEAD_V) for h in heads]
    rows = [slice(c * CHUNK, (c + 1) * CHUNK) for c in chunks]
    same_chunk = (lax.broadcasted_iota(jnp.int32, (rb, rb), 0) // CHUNK
                  == lax.broadcasted_iota(jnp.int32, (rb, rb), 1) // CHUNK)
    row_chunk = lax.broadcasted_iota(jnp.int32, (rb, GLA_HEAD_K), 0) // CHUNK
    ng = ng_ref[...]
    state = [st_ref[h] for h in heads]

    for blk in range(GLA_STEP_BLOCKS):
        rblk = slice(blk * rb, (blk + 1) * rb)
        lg = lg_ref[0, rblk, :]
        lg_hi = lg.astype(BF16)
        lg_lo = (lg - lg_hi.astype(F32)).astype(BF16)
        sums = jnp.dot(cm_ref[...], jnp.concatenate([lg_hi, lg_lo], axis=0),
                       preferred_element_type=F32)
        e_end = sums[:rb]
        dec_tot = jnp.exp(sums[rb:])
        q = q_ref[0, rblk, :] * (GLA_HEAD_K ** -0.5)
        k_dec = (k_ref[0, rblk, :] * jnp.exp(e_end)).astype(BF16)
        q_intra = q.astype(BF16)
        q_inter = (q * dec_tot).astype(BF16)
        v = v_ref[0, rblk, :].astype(BF16)
        r = r_ref[0, rblk, :]

        upd = []
        for h in heads:
            kh = k_dec[:, ks[h]]
            k_by_chunk = jnp.concatenate(
                [jnp.where(row_chunk == c, kh, jnp.zeros_like(kh)) for c in chunks], axis=1)
            upd.append(lax.dot_general(v[:, vs[h]], k_by_chunk, (((0,), (0,)), ((), ())),
                                       preferred_element_type=F32))
        scores = [lax.dot_general(q_intra[:, ks[h]], k_dec[:, ks[h]], (((1,), (1,)), ((), ())),
                                  preferred_element_type=F32) for h in heads]
        scores = [jnp.where(same_chunk, s, 0.0).astype(BF16) for s in scores]
        o_intra = [jnp.dot(scores[h], v[:, vs[h]], preferred_element_type=F32) for h in heads]
        o_inter = {}
        for h in heads:
            for c in chunks:
                o_inter[h, c] = lax.dot_general(q_inter[rows[c], ks[h]], state[h].astype(BF16),
                                                (((1,), (1,)), ((), ())),
                                                preferred_element_type=F32)
                state[h] = (state[h] * dec_tot[c * CHUNK:c * CHUNK + 1, ks[h]]
                            + upd[h][:, c * GLA_HEAD_K:(c + 1) * GLA_HEAD_K])
        for h in heads:
            o = o_intra[h] + jnp.concatenate([o_inter[h, c] for c in chunks], axis=0)
            o = o * lax.rsqrt(jnp.mean(o * o, axis=-1, keepdims=True) + RMS_EPS) * ng
            o_ref[0, rblk, vs[h]] = (_silu(r[:, vs[h]]) * o).astype(BF16)

    for h in heads:
        st_ref[h] = state[h]


def _gla_chunk_matrix():
    idx = jnp.arange(GLA_RB)
    same = (idx[:, None] // CHUNK) == (idx[None, :] // CHUNK)
    later = same & (idx[None, :] > idx[:, None])
    top = jnp.concatenate([later, later], axis=1)
    bot = jnp.concatenate([same, same], axis=1)
    return jnp.concatenate([top, bot], axis=0).astype(BF16)


def _gla_core(proj, log_g, norm_g, bsz, seq):
    proj3 = proj.reshape(bsz, seq, proj.shape[1])
    lg3 = log_g.reshape(bsz, seq, GLA_DK)
    rb = GLA_RB * GLA_STEP_BLOCKS
    return pl.pallas_call(
        _gla_kernel,
        grid=(bsz, seq // rb),
        in_specs=[
            pl.BlockSpec((1, rb, GLA_DK), lambda b, i: (b, i, 0)),
            pl.BlockSpec((1, rb, GLA_DK), lambda b, i: (b, i, 1)),
            pl.BlockSpec((1, rb, GLA_DV), lambda b, i: (b, i, 1)),
            pl.BlockSpec((1, rb, GLA_DV), lambda b, i: (b, i, 2)),
            pl.BlockSpec((1, rb, GLA_DK), lambda b, i: (b, i, 0)),
            pl.BlockSpec((2 * GLA_RB, 2 * GLA_RB), lambda b, i: (0, 0)),
            pl.BlockSpec((1, GLA_HEAD_V), lambda b, i: (0, 0)),
        ],
        out_specs=pl.BlockSpec((1, rb, GLA_DV), lambda b, i: (b, i, 0)),
        out_shape=jax.ShapeDtypeStruct((bsz, seq, GLA_DV), BF16),
        scratch_shapes=[pltpu.VMEM((GLA_HEADS, GLA_HEAD_V, GLA_HEAD_K), F32)],
        compiler_params=pltpu.CompilerParams(
            dimension_semantics=("parallel", "arbitrary"), vmem_limit_bytes=VMEM_LIMIT),
        name="gla_core",
    )(proj3, proj3, proj3, proj3, lg3, _gla_chunk_matrix(), norm_g.reshape(1, GLA_HEAD_V))


def _sb_kernel(q_ref, k_ref, v_ref, t_ref, o_ref):
    i = pl.program_id(2)
    qb, kb = SB_QB, SB_KB
    tmat = t_ref[...]
    pair_lanes = [slice(p * LANES, (p + 1) * LANES) for p in range(SB_PAIRS)]

    def split_heads(x):
        zero = jnp.zeros_like(x)
        keep = lax.broadcasted_iota(jnp.int32, x.shape, 1) < SB_HEAD
        return jnp.concatenate([jnp.where(keep, x, zero), jnp.where(keep, zero, x)], axis=0)

    def queries(lanes, row0, rows):
        return split_heads(q_ref[0, row0:row0 + rows, lanes] * (SB_HEAD ** -0.5))

    def scores(q2, lanes, start, nkeys):
        kwin = k_ref[0, pl.ds(start, nkeys), lanes]
        return lax.dot_general(q2, kwin, (((1,), (1,)), ((), ())),
                               preferred_element_type=F32)

    def gates(z, visible):
        if visible is not None:
            z = jnp.where(visible, z, SB_MASKED_LOGIT)
        log_beta = jnp.minimum(z, 0.0) - jnp.log(1.0 + jnp.exp2(jnp.abs(z) * -LOG2E))
        log_keep = log_beta - z
        lk_hi = log_keep.astype(BF16)
        lk_lo = (log_keep - lk_hi.astype(F32)).astype(BF16)
        return log_beta, jnp.concatenate([lk_hi, lk_lo], axis=1)

    def block_sums(lk):
        sums = jnp.dot(lk, tmat, preferred_element_type=F32)
        return sums[:, :kb], sums[:, kb:]

    def weights(log_beta, within, run):
        return jnp.exp(log_beta + within + run).astype(BF16)

    def attend(a_blocks, lanes, start):
        rows = a_blocks[0].shape[0] // 2
        vwin = v_ref[0, pl.ds(start, len(a_blocks) * kb), lanes]
        per_head = [jnp.dot(jnp.concatenate([a[hd * rows:(hd + 1) * rows] for a in a_blocks], axis=1),
                            vwin, preferred_element_type=F32) for hd in range(2)]
        keep = lax.broadcasted_iota(jnp.int32, (rows, LANES), 1) < SB_HEAD
        return jnp.where(keep, per_head[0], per_head[1])

    def sweep(lanes, j0, covered, run, acc):
        q2 = queries(lanes, 0, qb)
        s_loc = lax.broadcasted_iota(jnp.int32, (2 * qb, kb), 1)

        def cond(carry):
            j, run, _ = carry
            return jnp.logical_and(j >= 0, jnp.max(run) > SB_LOG_ZERO)

        def body(carry):
            j, run, acc = carry
            start = pl.multiple_of(j * kb, kb)
            fresh = start + s_loc < covered
            log_beta, lk = gates(scores(q2, lanes, start, kb), fresh)
            within, total = block_sums(lk)
            acc = acc + attend([weights(log_beta, within, run)], lanes, start)
            return j - 1, run + total, acc

        return lax.while_loop(cond, body, (j0, run, acc))[2]

    def window(rows, nblk):
        nslab = qb // rows
        units = [(p, h) for p in range(SB_PAIRS) for h in range(nslab)]
        win0 = [pl.multiple_of(jnp.maximum(i * qb + (h + 1) * rows - nblk * kb, 0), rows)
                for h in range(nslab)]
        r_loc = lax.broadcasted_iota(jnp.int32, (2 * rows, kb), 0) % rows
        s_loc = lax.broadcasted_iota(jnp.int32, (2 * rows, kb), 1)
        last_visible = [s_loc < r_loc + (i * qb + h * rows - win0[h] - (nblk - 1) * kb)
                        for h in range(nslab)]

        q2 = {(p, h): queries(pair_lanes[p], h * rows, rows) for p, h in units}
        z = {u: scores(q2[u], pair_lanes[u[0]], win0[u[1]], nblk * kb) for u in units}
        gate = {(u, w): gates(z[u][:, w * kb:(w + 1) * kb],
                              last_visible[u[1]] if w == nblk - 1 else None)
                for u in units for w in range(nblk)}
        sums = {uw: block_sums(gate[uw][1]) for uw in gate}
        run, acc = {}, {}
        for u in units:
            run[u] = jnp.zeros((2 * rows, kb), F32)
            a_blocks = [None] * nblk
            for w in reversed(range(nblk)):
                a_blocks[w] = weights(gate[u, w][0], sums[u, w][0], run[u])
                run[u] = run[u] + sums[u, w][1]
            acc[u] = attend(a_blocks, pair_lanes[u[0]], win0[u[1]])
        left = functools.reduce(jnp.maximum, run.values())
        left = jnp.max(jnp.max(left, axis=0, keepdims=True)[:, :1])

        def done():
            return tuple(jnp.concatenate([acc[p, h] for h in range(nslab)], axis=0)
                         for p in range(SB_PAIRS))

        def rest():
            covered = jnp.concatenate([jnp.full((rows, kb), 1, jnp.int32) * win0[h]
                                       for _ in range(2) for h in range(nslab)], axis=0)
            outs = []
            for p, lanes in enumerate(pair_lanes):
                run_p = jnp.concatenate([run[p, h][hd * rows:(hd + 1) * rows]
                                         for hd in range(2) for h in range(nslab)], axis=0)
                outs.append(sweep(lanes, j0, covered, run_p, done()[p]))
            return tuple(outs)

        j0 = (win0[-1] + kb - 1) // kb - 1
        more = jnp.logical_and(j0 >= 0, left > SB_LOG_ZERO)
        outs = lax.cond(more, rest, done)
        for p, lanes in enumerate(pair_lanes):
            o_ref[0, :, lanes] = outs[p].astype(o_ref.dtype)

    full = i * qb >= (SB_SLAB_BLOCKS - 1) * kb
    pl.when(full)(lambda: window(SB_SLAB, SB_SLAB_BLOCKS))
    pl.when(jnp.logical_not(full))(lambda: window(qb, 1))


def _sb_sum_matrix():
    j = jnp.arange(SB_KB)
    later = j[:, None] > j[None, :]
    half = jnp.concatenate([later, jnp.ones((SB_KB, SB_KB), bool)], axis=1)
    return jnp.concatenate([half, half], axis=0).astype(BF16)


def _sb_attn(q, kv, bsz, seq):
    q3 = q.reshape(bsz, seq, D_MODEL)
    kv3 = kv.reshape(bsz, seq, 2 * D_MODEL)
    width = SB_PAIRS * LANES
    groups = D_MODEL // width
    return pl.pallas_call(
        _sb_kernel,
        grid=(bsz, groups, seq // SB_QB),
        in_specs=[
            pl.BlockSpec((1, SB_QB, width), lambda b, p, i: (b, i, p)),
            pl.BlockSpec((1, seq, width), lambda b, p, i: (b, 0, p)),
            pl.BlockSpec((1, seq, width), lambda b, p, i: (b, 0, groups + p)),
            pl.BlockSpec((2 * SB_KB, 2 * SB_KB), lambda b, p, i: (0, 0)),
        ],
        out_specs=pl.BlockSpec((1, SB_QB, width), lambda b, p, i: (b, i, p)),
        out_shape=jax.ShapeDtypeStruct((bsz, seq, D_MODEL), BF16),
        compiler_params=pltpu.CompilerParams(
            dimension_semantics=("parallel", "parallel", "arbitrary"),
            vmem_limit_bytes=VMEM_LIMIT),
        name="sb_attn",
    )(q3, kv3, kv3, _sb_sum_matrix())


def kernel(x, ln_g, ln_b, ffn_w_up, ffn_w_down, gla_w_in, gla_w_gk, gla_b_gk, gla_norm_g,
           gla_w_out, sb_w_kv, sb_w_q, sb_w_out):
    bsz, seq, d = x.shape
    assert d == D_MODEL and seq % (GLA_RB * GLA_STEP_BLOCKS) == 0 and seq % SB_QB == 0 and (bsz * seq) % BLK_TM == 0
    assert SB_QB == SB_KB and SB_QB % SB_SLAB == 0 and DEPTH == 2 and N_A_LAYERS == 1
    x = x.reshape(bsz * seq, d)
    w_up = ffn_w_up.astype(BF16)
    w_down = ffn_w_down.astype(BF16)

    n_main = 2 * GLA_DK + 2 * GLA_DV
    w_in = gla_w_in[0]
    w_low = jnp.pad(w_in[:, n_main:], ((0, 0), (0, LANES - GATE_RANK))).astype(BF16)
    w_gk = jnp.pad(gla_w_gk[0], ((0, LANES - GATE_RANK), (0, 0))).astype(BF16)
    x, proj, log_g = _block(w_up, w_down, 0, 0, ln_g[0, 0], ln_b[0, 0], x=x,
                            post=("gla", w_in[:, :n_main].astype(BF16), w_low, w_gk, gla_b_gk[0]))
    y = _gla_core(proj, log_g, gla_norm_g[0], bsz, seq).reshape(bsz * seq, GLA_DV)
    x, kv = _block(w_up, w_down, 0, 1, ln_g[0, 2], ln_b[0, 2],
                   pre=(y, x, gla_w_out[0].astype(BF16), ln_g[0, 1], ln_b[0, 1]),
                   post=("matmul", sb_w_kv.astype(BF16)))

    x, q = _block(w_up, w_down, 1, 0, ln_g[1, 0], ln_b[1, 0], x=x,
                  post=("matmul", sb_w_q[0].astype(BF16)))
    y = _sb_attn(q, kv, bsz, seq).reshape(bsz * seq, D_MODEL)
    (x,) = _block(w_up, w_down, 1, 1, ln_g[1, 2], ln_b[1, 2],
                  pre=(y, x, sb_w_out[0].astype(BF16), ln_g[1, 1], ln_b[1, 1]))
    return x.reshape(bsz, seq, d)
```

```python
import functools

import jax
import jax.numpy as jnp
from jax import lax
from jax.experimental import pallas as pl
from jax.experimental.pallas import tpu as pltpu

F32 = jnp.float32
BF16 = jnp.bfloat16

D_MODEL = 1024
DEPTH = 2
CHUNK = 64
N_A_LAYERS = DEPTH // 2
D_FF = 2816
GLA_HEADS = 4
GLA_DK = D_MODEL // 2
GLA_DV = D_MODEL
GLA_HEAD_K = GLA_DK // GLA_HEADS
GLA_HEAD_V = GLA_DV // GLA_HEADS
GATE_RANK = 16
GATE_TAU = 16.0
SB_HEADS = 16
SB_HEAD = D_MODEL // SB_HEADS
DEEPNORM_ALPHA = (2 * DEPTH) ** 0.25
LN_EPS = 1e-5
RMS_EPS = 1e-6

LANES = 128
VMEM_LIMIT = 48 * 1024 * 1024
VMEM_TEMP_BYTES = 10 * 1024 * 1024

BLK_TM = 512
BLK_LN_ROWS = 256
FFN_TF = 256
GLA_RB = 256
SB_QB = 128
SB_KB = 128
SB_SLAB = 64
SB_SLAB_BLOCKS = 2
SB_PAIRS = 8
SB_LOG_ZERO = -104.0
SB_MASKED_LOGIT = -1e30
LOG2E = 1.4426950408889634


def _layer_norm(y, g, b):
    mu = jnp.mean(y, axis=-1, keepdims=True)
    yc = y - mu
    var = jnp.mean(yc * yc, axis=-1, keepdims=True)
    return yc * lax.rsqrt(var + LN_EPS) * g + b


def _silu(x):
    return x * jax.nn.sigmoid(x)


def _block_kernel(*refs, pre, post, seq_tiles):
    refs = list(refs)

    def take(n):
        out, refs[:] = refs[:n], refs[n:]
        return out

    if pre:
        y_ref, xres_ref, wo_ref, g1_ref, b1_ref = take(5)
    else:
        (x_ref,) = take(1)
    wu_ref, wd_ref, g2_ref, b2_ref = take(4)
    if post == "gla":
        wm_ref, wl_ref, wgk_ref, bgk_ref, cm_ref, ng_ref = take(6)
    elif post == "matmul":
        (wp_ref,) = take(1)
    (o_ref,) = take(1)
    if post:
        (p_ref,) = take(1)
    (act_ref,) = take(1)
    if pre:
        (x_ref,) = take(1)
    if post:
        (ob_ref,) = take(1)
    if post == "gla":
        (st_ref,) = take(1)

        @pl.when(pl.program_id(0) % seq_tiles == 0)
        def _():
            st_ref[...] = jnp.zeros_like(st_ref)
    assert not refs

    row_blocks = [slice(r * BLK_LN_ROWS, (r + 1) * BLK_LN_ROWS)
                  for r in range(BLK_TM // BLK_LN_ROWS)]
    if pre:
        for rows in row_blocks:
            mix = jnp.dot(y_ref[rows, :], wo_ref[...], preferred_element_type=F32)
            x_ref[rows, :] = _layer_norm(DEEPNORM_ALPHA * xres_ref[rows, :] + mix,
                                         g1_ref[...], b1_ref[...])

    for rows in (row_blocks if pre else [slice(0, BLK_TM)]):
        xb = x_ref[rows, :].astype(BF16)
        for c in range(D_FF // FFN_TF):
            cols = slice(c * FFN_TF, (c + 1) * FFN_TF)
            gate = jnp.dot(xb, wu_ref[:, cols], preferred_element_type=F32)
            up = jnp.dot(xb, wu_ref[:, D_FF + c * FFN_TF:D_FF + (c + 1) * FFN_TF],
                         preferred_element_type=F32)
            act_ref[rows, cols] = (_silu(gate) * up).astype(BF16)

    for rows in row_blocks:
        down = jnp.dot(act_ref[rows, :], wd_ref[...], preferred_element_type=F32)
        out = _layer_norm(DEEPNORM_ALPHA * x_ref[rows, :] + 0.5 * down, g2_ref[...], b2_ref[...])
        o_ref[rows, :] = out
        if post:
            ob_ref[rows, :] = out.astype(BF16)

    if post == "matmul":
        for rows in row_blocks:
            p_ref[rows, :] = jnp.dot(ob_ref[rows, :], wp_ref[...],
                                     preferred_element_type=F32).astype(p_ref.dtype)
    elif post == "gla":
        state = [st_ref[h] for h in range(GLA_HEADS)]
        for rows in row_blocks:
            ob = ob_ref[rows, :]
            proj = jnp.dot(ob, wm_ref[...], preferred_element_type=F32)
            low = jnp.dot(ob, wl_ref[...], preferred_element_type=F32)
            gate_pre = jnp.dot(low.astype(BF16), wgk_ref[...],
                               preferred_element_type=F32) + bgk_ref[...]
            log_g = jax.nn.log_sigmoid(gate_pre) / GATE_TAU
            q, k = proj[:, :GLA_DK], proj[:, GLA_DK:2 * GLA_DK]
            v, r = proj[:, 2 * GLA_DK:2 * GLA_DK + GLA_DV], proj[:, 2 * GLA_DK + GLA_DV:]
            outs, state = _gla_block(q, k, v, r, log_g, cm_ref[...], ng_ref[...], state)
            for h, out in enumerate(outs):
                p_ref[rows, h * GLA_HEAD_V:(h + 1) * GLA_HEAD_V] = out.astype(p_ref.dtype)
        for h in range(GLA_HEADS):
            st_ref[h] = state[h]


def _nbytes(shape, dtype):
    n = jnp.dtype(dtype).itemsize
    for s in shape:
        n *= s
    return n


def _block(w_up, w_down, layer, half, g2, b2, *, x=None, pre=None, post=None):
    d = D_MODEL
    m = (pre[1] if pre else x).shape[0]
    resident = pl.Buffered(1)
    vmem = [0]

    def tile(n, dtype):
        vmem[0] += 2 * _nbytes((BLK_TM, n), dtype)
        return pl.BlockSpec((BLK_TM, n), lambda i: (i, 0))

    def whole(a):
        vmem[0] += _nbytes(a.shape, a.dtype)
        return pl.BlockSpec(a.shape, lambda i: (0,) * a.ndim, pipeline_mode=resident)

    def stacked(a):
        vmem[0] += _nbytes(a.shape[2:], a.dtype)
        return pl.BlockSpec((None, None) + a.shape[2:], lambda i: (layer, half, 0, 0),
                            pipeline_mode=resident)

    def scratch(n, dtype):
        vmem[0] += _nbytes((BLK_TM, n), dtype)
        return pltpu.VMEM((BLK_TM, n), dtype)

    args, in_specs = [], []
    if pre:
        y, xres, w_out, g1, b1 = pre
        args += [y, xres, w_out, g1.reshape(1, d), b1.reshape(1, d)]
        in_specs += [tile(y.shape[1], y.dtype), tile(d, F32)] + [whole(a) for a in args[2:]]
    else:
        args += [x]
        in_specs += [tile(d, F32)]
    vecs = [g2.reshape(1, d), b2.reshape(1, d)]
    args += [w_up, w_down] + vecs
    in_specs += [stacked(w_up), stacked(w_down)] + [whole(a) for a in vecs]
    out_shape = [jax.ShapeDtypeStruct((m, d), F32)]
    out_specs = [tile(d, F32)]
    kind = post[0] if post else None
    seq_tiles = None
    if kind == "matmul":
        w_p = post[1]
        args += [w_p]
        in_specs += [whole(w_p)]
        out_shape += [jax.ShapeDtypeStruct((m, w_p.shape[1]), BF16)]
        out_specs += [tile(w_p.shape[1], BF16)]
    elif kind == "gla":
        w_in, w_low, w_gk, b_gk, norm_g, seq = post[1:]
        assert BLK_LN_ROWS == GLA_RB and seq % BLK_TM == 0
        seq_tiles = seq // BLK_TM
        n_main = 2 * GLA_DK + 2 * GLA_DV
        extra = [w_low, w_gk, b_gk.reshape(1, GLA_DK), _gla_chunk_matrix(),
                 norm_g.reshape(1, GLA_HEAD_V)]
        args += [w_in] + extra
        vmem[0] += _nbytes((d, n_main), w_in.dtype)
        in_specs += [pl.BlockSpec((d, n_main), lambda i: (0, 0), pipeline_mode=resident)]
        in_specs += [whole(a) for a in extra]
        out_shape += [jax.ShapeDtypeStruct((m, GLA_DV), BF16)]
        out_specs += [tile(GLA_DV, BF16)]
    scratch_shapes = [scratch(D_FF, BF16)]
    if pre:
        scratch_shapes += [scratch(d, F32)]
    if post:
        scratch_shapes += [scratch(d, BF16)]
    if kind == "gla":
        state_shape = (GLA_HEADS, GLA_HEAD_V, GLA_HEAD_K)
        vmem[0] += _nbytes(state_shape, F32)
        scratch_shapes += [pltpu.VMEM(state_shape, F32)]
    return pl.pallas_call(
        functools.partial(_block_kernel, pre=bool(pre), post=kind, seq_tiles=seq_tiles),
        grid=(m // BLK_TM,),
        in_specs=in_specs,
        out_specs=out_specs,
        out_shape=out_shape,
        scratch_shapes=scratch_shapes,
        compiler_params=pltpu.CompilerParams(
            dimension_semantics=("arbitrary" if kind == "gla" else "parallel",),
            vmem_limit_bytes=vmem[0] + VMEM_TEMP_BYTES),
        name="block" + ("_pre" if pre else "") + ("_" + kind if kind else ""),
    )(*args)


def _gla_block(q, k, v, r, lg, chunk_matrix, ng, state):
    rb = GLA_RB
    heads = range(GLA_HEADS)
    chunks = range(rb // CHUNK)
    ks = [slice(h * GLA_HEAD_K, (h + 1) * GLA_HEAD_K) for h in heads]
    vs = [slice(h * GLA_HEAD_V, (h + 1) * GLA_HEAD_V) for h in heads]
    rows = [slice(c * CHUNK, (c + 1) * CHUNK) for c in chunks]
    same_chunk = (lax.broadcasted_iota(jnp.int32, (rb, rb), 0) // CHUNK
                  == lax.broadcasted_iota(jnp.int32, (rb, rb), 1) // CHUNK)
    row_chunk = lax.broadcasted_iota(jnp.int32, (rb, GLA_HEAD_K), 0) // CHUNK
    state = list(state)

    lg_hi = lg.astype(BF16)
    lg_lo = (lg - lg_hi.astype(F32)).astype(BF16)
    sums = jnp.dot(chunk_matrix, jnp.concatenate([lg_hi, lg_lo], axis=0),
                   preferred_element_type=F32)
    e_end = sums[:rb]
    dec_tot = jnp.exp(sums[rb:])
    q = q * (GLA_HEAD_K ** -0.5)
    k_dec = (k * jnp.exp(e_end)).astype(BF16)
    q_intra = q.astype(BF16)
    q_inter = (q * dec_tot).astype(BF16)
    v = v.astype(BF16)

    upd = []
    for h in heads:
        kh = k_dec[:, ks[h]]
        k_by_chunk = jnp.concatenate(
            [jnp.where(row_chunk == c, kh, jnp.zeros_like(kh)) for c in chunks], axis=1)
        upd.append(lax.dot_general(v[:, vs[h]], k_by_chunk, (((0,), (0,)), ((), ())),
                                   preferred_element_type=F32))
    scores = [lax.dot_general(q_intra[:, ks[h]], k_dec[:, ks[h]], (((1,), (1,)), ((), ())),
                              preferred_element_type=F32) for h in heads]
    scores = [jnp.where(same_chunk, s, 0.0).astype(BF16) for s in scores]
    o_intra = [jnp.dot(scores[h], v[:, vs[h]], preferred_element_type=F32) for h in heads]
    o_inter = {}
    for h in heads:
        for c in chunks:
            o_inter[h, c] = lax.dot_general(q_inter[rows[c], ks[h]], state[h].astype(BF16),
                                            (((1,), (1,)), ((), ())),
                                            preferred_element_type=F32)
            state[h] = (state[h] * dec_tot[c * CHUNK:c * CHUNK + 1, ks[h]]
                        + upd[h][:, c * GLA_HEAD_K:(c + 1) * GLA_HEAD_K])
    outs = []
    for h in heads:
        o = o_intra[h] + jnp.concatenate([o_inter[h, c] for c in chunks], axis=0)
        o = o * lax.rsqrt(jnp.mean(o * o, axis=-1, keepdims=True) + RMS_EPS) * ng
        outs.append(_silu(r[:, vs[h]]) * o)
    return outs, state


def _gla_chunk_matrix():
    idx = jnp.arange(GLA_RB)
    same = (idx[:, None] // CHUNK) == (idx[None, :] // CHUNK)
    later = same & (idx[None, :] > idx[:, None])
    top = jnp.concatenate([later, later], axis=1)
    bot = jnp.concatenate([same, same], axis=1)
    return jnp.concatenate([top, bot], axis=0).astype(BF16)


def _sb_kernel(q_ref, k_ref, v_ref, t_ref, o_ref):
    i = pl.program_id(2)
    qb, kb = SB_QB, SB_KB
    tmat = t_ref[...]
    pair_lanes = [slice(p * LANES, (p + 1) * LANES) for p in range(SB_PAIRS)]

    def split_heads(x):
        zero = jnp.zeros_like(x)
        keep = lax.broadcasted_iota(jnp.int32, x.shape, 1) < SB_HEAD
        return jnp.concatenate([jnp.where(keep, x, zero), jnp.where(keep, zero, x)], axis=0)

    def queries(lanes, row0, rows):
        return split_heads(q_ref[0, row0:row0 + rows, lanes] * (SB_HEAD ** -0.5))

    def scores(q2, lanes, start, nkeys):
        kwin = k_ref[0, pl.ds(start, nkeys), lanes]
        return lax.dot_general(q2, kwin, (((1,), (1,)), ((), ())),
                               preferred_element_type=F32)

    def gates(z, visible):
        if visible is not None:
            z = jnp.where(visible, z, SB_MASKED_LOGIT)
        log_beta = jnp.minimum(z, 0.0) - jnp.log(1.0 + jnp.exp2(jnp.abs(z) * -LOG2E))
        log_keep = log_beta - z
        lk_hi = log_keep.astype(BF16)
        lk_lo = (log_keep - lk_hi.astype(F32)).astype(BF16)
        return log_beta, jnp.concatenate([lk_hi, lk_lo], axis=1)

    def block_sums(lk):
        sums = jnp.dot(lk, tmat, preferred_element_type=F32)
        return sums[:, :kb], sums[:, kb:]

    def weights(log_beta, within, run):
        return jnp.exp(log_beta + within + run).astype(BF16)

    def attend(a_blocks, lanes, start):
        rows = a_blocks[0].shape[0] // 2
        vwin = v_ref[0, pl.ds(start, len(a_blocks) * kb), lanes]
        per_head = [jnp.dot(jnp.concatenate([a[hd * rows:(hd + 1) * rows] for a in a_blocks], axis=1),
                            vwin, preferred_element_type=F32) for hd in range(2)]
        keep = lax.broadcasted_iota(jnp.int32, (rows, LANES), 1) < SB_HEAD
        return jnp.where(keep, per_head[0], per_head[1])

    def sweep(lanes, j0, covered, run, acc):
        q2 = queries(lanes, 0, qb)
        s_loc = lax.broadcasted_iota(jnp.int32, (2 * qb, kb), 1)

        def cond(carry):
            j, run, _ = carry
            return jnp.logical_and(j >= 0, jnp.max(run) > SB_LOG_ZERO)

        def body(carry):
            j, run, acc = carry
            start = pl.multiple_of(j * kb, kb)
            fresh = start + s_loc < covered
            log_beta, lk = gates(scores(q2, lanes, start, kb), fresh)
            within, total = block_sums(lk)
            acc = acc + attend([weights(log_beta, within, run)], lanes, start)
            return j - 1, run + total, acc

        return lax.while_loop(cond, body, (j0, run, acc))[2]

    def window(rows, nblk):
        nslab = qb // rows
        units = [(p, h) for p in range(SB_PAIRS) for h in range(nslab)]
        win0 = [pl.multiple_of(jnp.maximum(i * qb + (h + 1) * rows - nblk * kb, 0), rows)
                for h in range(nslab)]
        r_loc = lax.broadcasted_iota(jnp.int32, (2 * rows, kb), 0) % rows
        s_loc = lax.broadcasted_iota(jnp.int32, (2 * rows, kb), 1)
        last_visible = [s_loc < r_loc + (i * qb + h * rows - win0[h] - (nblk - 1) * kb)
                        for h in range(nslab)]

        q2 = {(p, h): queries(pair_lanes[p], h * rows, rows) for p, h in units}
        z = {u: scores(q2[u], pair_lanes[u[0]], win0[u[1]], nblk * kb) for u in units}
        gate = {(u, w): gates(z[u][:, w * kb:(w + 1) * kb],
                              last_visible[u[1]] if w == nblk - 1 else None)
                for u in units for w in range(nblk)}
        sums = {uw: block_sums(gate[uw][1]) for uw in gate}
        run, acc = {}, {}
        for u in units:
            run[u] = jnp.zeros((2 * rows, kb), F32)
            a_blocks = [None] * nblk
            for w in reversed(range(nblk)):
                a_blocks[w] = weights(gate[u, w][0], sums[u, w][0], run[u])
                run[u] = run[u] + sums[u, w][1]
            acc[u] = attend(a_blocks, pair_lanes[u[0]], win0[u[1]])
        left = functools.reduce(jnp.maximum, run.values())
        left = jnp.max(jnp.max(left, axis=0, keepdims=True)[:, :1])

        def done():
            return tuple(jnp.concatenate([acc[p, h] for h in range(nslab)], axis=0)
                         for p in range(SB_PAIRS))

        def rest():
            covered = jnp.concatenate([jnp.full((rows, kb), 1, jnp.int32) * win0[h]
                                       for _ in range(2) for h in range(nslab)], axis=0)
            outs = []
            for p, lanes in enumerate(pair_lanes):
                run_p = jnp.concatenate([run[p, h][hd * rows:(hd + 1) * rows]
                                         for hd in range(2) for h in range(nslab)], axis=0)
                outs.append(sweep(lanes, j0, covered, run_p, done()[p]))
            return tuple(outs)

        j0 = (win0[-1] + kb - 1) // kb - 1
        more = jnp.logical_and(j0 >= 0, left > SB_LOG_ZERO)
        outs = lax.cond(more, rest, done)
        for p, lanes in enumerate(pair_lanes):
            o_ref[0, :, lanes] = outs[p].astype(o_ref.dtype)

    full = i * qb >= (SB_SLAB_BLOCKS - 1) * kb
    pl.when(full)(lambda: window(SB_SLAB, SB_SLAB_BLOCKS))
    pl.when(jnp.logical_not(full))(lambda: window(qb, 1))


def _sb_sum_matrix():
    j = jnp.arange(SB_KB)
    later = j[:, None] > j[None, :]
    half = jnp.concatenate([later, jnp.ones((SB_KB, SB_KB), bool)], axis=1)
    return jnp.concatenate([half, half], axis=0).astype(BF16)


def _sb_attn(q, kv, bsz, seq):
    q3 = q.reshape(bsz, seq, D_MODEL)
    kv3 = kv.reshape(bsz, seq, 2 * D_MODEL)
    width = SB_PAIRS * LANES
    groups = D_MODEL // width
    return pl.pallas_call(
        _sb_kernel,
        grid=(bsz, groups, seq // SB_QB),
        in_specs=[
            pl.BlockSpec((1, SB_QB, width), lambda b, p, i: (b, i, p)),
            pl.BlockSpec((1, seq, width), lambda b, p, i: (b, 0, p)),
            pl.BlockSpec((1, seq, width), lambda b, p, i: (b, 0, groups + p)),
            pl.BlockSpec((2 * SB_KB, 2 * SB_KB), lambda b, p, i: (0, 0)),
        ],
        out_specs=pl.BlockSpec((1, SB_QB, width), lambda b, p, i: (b, i, p)),
        out_shape=jax.ShapeDtypeStruct((bsz, seq, D_MODEL), BF16),
        compiler_params=pltpu.CompilerParams(
            dimension_semantics=("parallel", "parallel", "arbitrary"),
            vmem_limit_bytes=VMEM_LIMIT),
        name="sb_attn",
    )(q3, kv3, kv3, _sb_sum_matrix())


def kernel(x, ln_g, ln_b, ffn_w_up, ffn_w_down, gla_w_in, gla_w_gk, gla_b_gk, gla_norm_g,
           gla_w_out, sb_w_kv, sb_w_q, sb_w_out):
    bsz, seq, d = x.shape
    assert d == D_MODEL and seq % SB_QB == 0 and (bsz * seq) % BLK_TM == 0
    assert SB_QB == SB_KB and SB_QB % SB_SLAB == 0 and DEPTH == 2 and N_A_LAYERS == 1
    x = x.reshape(bsz * seq, d)
    w_up = ffn_w_up.astype(BF16)
    w_down = ffn_w_down.astype(BF16)

    w_in = gla_w_in[0].astype(BF16)
    w_low = jnp.pad(w_in[:, -GATE_RANK:], ((0, 0), (0, LANES - GATE_RANK)))
    w_gk = jnp.pad(gla_w_gk[0], ((0, LANES - GATE_RANK), (0, 0))).astype(BF16)
    x, y = _block(w_up, w_down, 0, 0, ln_g[0, 0], ln_b[0, 0], x=x,
                  post=("gla", w_in, w_low, w_gk, gla_b_gk[0], gla_norm_g[0], seq))
    x, kv = _block(w_up, w_down, 0, 1, ln_g[0, 2], ln_b[0, 2],
                   pre=(y, x, gla_w_out[0].astype(BF16), ln_g[0, 1], ln_b[0, 1]),
                   post=("matmul", sb_w_kv.astype(BF16)))

    x, q = _block(w_up, w_down, 1, 0, ln_g[1, 0], ln_b[1, 0], x=x,
                  post=("matmul", sb_w_q[0].astype(BF16)))
    y = _sb_attn(q, kv, bsz, seq).reshape(bsz * seq, D_MODEL)
    (x,) = _block(w_up, w_down, 1, 1, ln_g[1, 2], ln_b[1, 2],
                  pre=(y, x, sb_w_out[0].astype(BF16), ln_g[1, 1], ln_b[1, 1]))
    return x.reshape(bsz, seq, d)
```

```python
import functools

import jax
import jax.numpy as jnp
from jax import lax
from jax.experimental import pallas as pl
from jax.experimental.pallas import tpu as pltpu

F32 = jnp.float32
BF16 = jnp.bfloat16

D_MODEL = 1024
DEPTH = 2
CHUNK = 64
N_A_LAYERS = DEPTH // 2
D_FF = 2816
GLA_HEADS = 4
GLA_DK = D_MODEL // 2
GLA_DV = D_MODEL
GLA_HEAD_K = GLA_DK // GLA_HEADS
GLA_HEAD_V = GLA_DV // GLA_HEADS
GATE_RANK = 16
GATE_TAU = 16.0
SB_HEADS = 16
SB_HEAD = D_MODEL // SB_HEADS
DEEPNORM_ALPHA = (2 * DEPTH) ** 0.25
LN_EPS = 1e-5
RMS_EPS = 1e-6

LANES = 128
VMEM_LIMIT = 48 * 1024 * 1024
VMEM_TEMP_BYTES = 10 * 1024 * 1024

BLK_TM = 512
BLK_LN_ROWS = 256
FFN_TF = 256
FFN_W_STEPS = 11
FFN_WU_COLS = 2 * D_FF // FFN_W_STEPS
FFN_WD_ROWS = D_FF // FFN_W_STEPS
GLA_RB = 256
SB_QB = 128
SB_KB = 128
SB_SLAB = 64
SB_SLAB_BLOCKS = 2
SB_PAIRS = 8
SB_LOG_ZERO = -104.0
SB_MASKED_LOGIT = -1e30
LOG2E = 1.4426950408889634


def _layer_norm(y, g, b):
    mu = jnp.mean(y, axis=-1, keepdims=True)
    yc = y - mu
    var = jnp.mean(yc * yc, axis=-1, keepdims=True)
    return yc * lax.rsqrt(var + LN_EPS) * g + b


def _silu(x):
    return x * jax.nn.sigmoid(x)


def _block_kernel(*refs, pre, post, seq_tiles):
    step = pl.program_id(0)
    refs = list(refs)

    def take(n):
        out, refs[:] = refs[:n], refs[n:]
        return out

    if pre:
        y_ref, xres_ref, wo_ref, g1_ref, b1_ref = take(5)
    else:
        (x_ref,) = take(1)
    wu_in_ref, wd_in_ref, g2_ref, b2_ref = take(4)
    if post == "gla":
        wm_ref, wl_ref, wgk_ref, bgk_ref, cm_ref, ng_ref = take(6)
    elif post == "matmul":
        (wp_ref,) = take(1)
    (o_ref,) = take(1)
    if post:
        (p_ref,) = take(1)
    wu_ref, wd_ref, act_ref = take(3)
    if pre:
        (x_ref,) = take(1)
    if post:
        (ob_ref,) = take(1)
    if post == "gla":
        (st_ref,) = take(1)
    assert not refs

    @pl.when(step < FFN_W_STEPS)
    def _():
        wu_ref[step] = wu_in_ref[...].astype(BF16)
        rows = pl.ds(pl.multiple_of(step * FFN_WD_ROWS, FFN_WD_ROWS), FFN_WD_ROWS)
        wd_ref[rows, :] = wd_in_ref[...].astype(BF16)

    pl.when(step >= FFN_W_STEPS)(functools.partial(
        _block_tile, locals(), pre=pre, post=post, first_of_sequence=(
            (step - FFN_W_STEPS) % seq_tiles == 0 if post == "gla" else None)))


def _w_up_cols(wu_ref, col0):
    off = col0 % FFN_WU_COLS
    return wu_ref[col0 // FFN_WU_COLS, :, off:off + FFN_TF]


def _block_tile(r, *, pre, post, first_of_sequence):
    x_ref, o_ref, act_ref, wu_ref, wd_ref = (r[k] for k in
                                             ("x_ref", "o_ref", "act_ref", "wu_ref", "wd_ref"))
    g2_ref, b2_ref = r["g2_ref"], r["b2_ref"]
    if pre:
        y_ref, xres_ref, wo_ref, g1_ref, b1_ref = (r[k] for k in
                                                   ("y_ref", "xres_ref", "wo_ref", "g1_ref", "b1_ref"))
    if post:
        p_ref, ob_ref = r["p_ref"], r["ob_ref"]
    if post == "matmul":
        wp_ref = r["wp_ref"]
    if post == "gla":
        wm_ref, wl_ref, wgk_ref, bgk_ref, cm_ref, ng_ref, st_ref = (r[k] for k in (
            "wm_ref", "wl_ref", "wgk_ref", "bgk_ref", "cm_ref", "ng_ref", "st_ref"))

        @pl.when(first_of_sequence)
        def _():
            st_ref[...] = jnp.zeros_like(st_ref)

    row_blocks = [slice(r * BLK_LN_ROWS, (r + 1) * BLK_LN_ROWS)
                  for r in range(BLK_TM // BLK_LN_ROWS)]
    if pre:
        for rows in row_blocks:
            mix = jnp.dot(y_ref[rows, :], wo_ref[...], preferred_element_type=F32)
            x_ref[rows, :] = _layer_norm(DEEPNORM_ALPHA * xres_ref[rows, :] + mix,
                                         g1_ref[...], b1_ref[...])

    for rows in (row_blocks if pre else [slice(0, BLK_TM)]):
        xb = x_ref[rows, :].astype(BF16)
        for c in range(D_FF // FFN_TF):
            cols = slice(c * FFN_TF, (c + 1) * FFN_TF)
            gate = jnp.dot(xb, _w_up_cols(wu_ref, c * FFN_TF), preferred_element_type=F32)
            up = jnp.dot(xb, _w_up_cols(wu_ref, D_FF + c * FFN_TF), preferred_element_type=F32)
            act_ref[rows, cols] = (_silu(gate) * up).astype(BF16)

    for rows in row_blocks:
        down = jnp.dot(act_ref[rows, :], wd_ref[...], preferred_element_type=F32)
        out = _layer_norm(DEEPNORM_ALPHA * x_ref[rows, :] + 0.5 * down, g2_ref[...], b2_ref[...])
        o_ref[rows, :] = out
        if post:
            ob_ref[rows, :] = out.astype(BF16)

    if post == "matmul":
        for rows in row_blocks:
            p_ref[rows, :] = jnp.dot(ob_ref[rows, :], wp_ref[...],
                                     preferred_element_type=F32).astype(p_ref.dtype)
    elif post == "gla":
        state = [st_ref[h] for h in range(GLA_HEADS)]
        obs = [ob_ref[rows, :] for rows in row_blocks]
        lows = [jnp.dot(ob, wl_ref[...], preferred_element_type=F32) for ob in obs]
        gate_pre = [jnp.dot(low.astype(BF16), wgk_ref[...], preferred_element_type=F32)
                    + bgk_ref[...] for low in lows]
        projs = [jnp.dot(ob, wm_ref[...], preferred_element_type=F32) for ob in obs]
        local = []
        for proj, pre_g in zip(projs, gate_pre):
            log_g = jax.nn.log_sigmoid(pre_g) / GATE_TAU
            q, k = proj[:, :GLA_DK], proj[:, GLA_DK:2 * GLA_DK]
            v = proj[:, 2 * GLA_DK:2 * GLA_DK + GLA_DV]
            local.append(_gla_local(q, k, v, log_g, cm_ref[...]))
        for rows, proj, loc in zip(row_blocks, projs, local):
            outs, state = _gla_carry(loc, proj[:, 2 * GLA_DK + GLA_DV:], ng_ref[...], state)
            for h, out in enumerate(outs):
                p_ref[rows, h * GLA_HEAD_V:(h + 1) * GLA_HEAD_V] = out.astype(p_ref.dtype)
        for h in range(GLA_HEADS):
            st_ref[h] = state[h]


def _nbytes(shape, dtype):
    n = jnp.dtype(dtype).itemsize
    for s in shape:
        n *= s
    return n


def _block(w_up, w_down, layer, half, g2, b2, *, x=None, pre=None, post=None):
    d = D_MODEL
    m = (pre[1] if pre else x).shape[0]
    resident = pl.Buffered(1)
    vmem = [0]

    def tile(n, dtype):
        vmem[0] += 2 * _nbytes((BLK_TM, n), dtype)
        return pl.BlockSpec((BLK_TM, n), lambda s: (jnp.maximum(s - FFN_W_STEPS, 0), 0))

    def whole(a):
        vmem[0] += _nbytes(a.shape, a.dtype)
        return pl.BlockSpec(a.shape, lambda i: (0,) * a.ndim, pipeline_mode=resident)

    def weight_chunks(a, rows, cols, axis):
        vmem[0] += 2 * _nbytes((rows, cols), a.dtype)
        last = FFN_W_STEPS - 1
        if axis == 0:
            return pl.BlockSpec((None, None, rows, cols),
                                lambda s: (layer, half, jnp.minimum(s, last), 0))
        return pl.BlockSpec((None, None, rows, cols),
                            lambda s: (layer, half, 0, jnp.minimum(s, last)))

    def scratch(n, dtype):
        vmem[0] += _nbytes((BLK_TM, n), dtype)
        return pltpu.VMEM((BLK_TM, n), dtype)

    args, in_specs = [], []
    if pre:
        y, xres, w_out, g1, b1 = pre
        args += [y, xres, w_out, g1.reshape(1, d), b1.reshape(1, d)]
        in_specs += [tile(y.shape[1], y.dtype), tile(d, F32)] + [whole(a) for a in args[2:]]
    else:
        args += [x]
        in_specs += [tile(d, F32)]
    vecs = [g2.reshape(1, d), b2.reshape(1, d)]
    args += [w_up, w_down] + vecs
    in_specs += [weight_chunks(w_up, d, FFN_WU_COLS, 1), weight_chunks(w_down, FFN_WD_ROWS, d, 0)]
    in_specs += [whole(a) for a in vecs]
    out_shape = [jax.ShapeDtypeStruct((m, d), F32)]
    out_specs = [tile(d, F32)]
    kind = post[0] if post else None
    seq_tiles = None
    if kind == "matmul":
        w_p = post[1]
        args += [w_p]
        in_specs += [whole(w_p)]
        out_shape += [jax.ShapeDtypeStruct((m, w_p.shape[1]), BF16)]
        out_specs += [tile(w_p.shape[1], BF16)]
    elif kind == "gla":
        w_in, w_low, w_gk, b_gk, norm_g, seq = post[1:]
        assert BLK_LN_ROWS == GLA_RB and seq % BLK_TM == 0
        seq_tiles = seq // BLK_TM
        n_main = 2 * GLA_DK + 2 * GLA_DV
        extra = [w_low, w_gk, b_gk.reshape(1, GLA_DK), _gla_chunk_matrix(),
                 norm_g.reshape(1, GLA_HEAD_V)]
        args += [w_in] + extra
        vmem[0] += _nbytes((d, n_main), w_in.dtype)
        in_specs += [pl.BlockSpec((d, n_main), lambda i: (0, 0), pipeline_mode=resident)]
        in_specs += [whole(a) for a in extra]
        out_shape += [jax.ShapeDtypeStruct((m, GLA_DV), BF16)]
        out_specs += [tile(GLA_DV, BF16)]
    vmem[0] += _nbytes((d, 2 * D_FF), BF16) + _nbytes((D_FF, d), BF16)
    scratch_shapes = [pltpu.VMEM((FFN_W_STEPS, d, FFN_WU_COLS), BF16), pltpu.VMEM((D_FF, d), BF16),
                      scratch(D_FF, BF16)]
    if pre:
        scratch_shapes += [scratch(d, F32)]
    if post:
        scratch_shapes += [scratch(d, BF16)]
    if kind == "gla":
        state_shape = (GLA_HEADS, GLA_HEAD_V, GLA_HEAD_K)
        vmem[0] += _nbytes(state_shape, F32)
        scratch_shapes += [pltpu.VMEM(state_shape, F32)]
    return pl.pallas_call(
        functools.partial(_block_kernel, pre=bool(pre), post=kind, seq_tiles=seq_tiles),
        grid=(FFN_W_STEPS + m // BLK_TM,),
        in_specs=in_specs,
        out_specs=out_specs,
        out_shape=out_shape,
        scratch_shapes=scratch_shapes,
        compiler_params=pltpu.CompilerParams(
            dimension_semantics=("arbitrary",),
            vmem_limit_bytes=vmem[0] + VMEM_TEMP_BYTES),
        name="block" + ("_pre" if pre else "") + ("_" + kind if kind else ""),
    )(*args)


def _gla_local(q, k, v, lg, chunk_matrix):
    rb = GLA_RB
    heads = range(GLA_HEADS)
    chunks = range(rb // CHUNK)
    ks = [slice(h * GLA_HEAD_K, (h + 1) * GLA_HEAD_K) for h in heads]
    vs = [slice(h * GLA_HEAD_V, (h + 1) * GLA_HEAD_V) for h in heads]
    same_chunk = (lax.broadcasted_iota(jnp.int32, (rb, rb), 0) // CHUNK
                  == lax.broadcasted_iota(jnp.int32, (rb, rb), 1) // CHUNK)
    row_chunk = lax.broadcasted_iota(jnp.int32, (rb, GLA_HEAD_K), 0) // CHUNK

    lg_hi = lg.astype(BF16)
    lg_lo = (lg - lg_hi.astype(F32)).astype(BF16)
    sums = jnp.dot(chunk_matrix, jnp.concatenate([lg_hi, lg_lo], axis=0),
                   preferred_element_type=F32)
    e_end = sums[:rb]
    dec_tot = jnp.exp(sums[rb:])
    q = q * (GLA_HEAD_K ** -0.5)
    k_dec = (k * jnp.exp(e_end)).astype(BF16)
    q_intra = q.astype(BF16)
    q_inter = (q * dec_tot).astype(BF16)
    v = v.astype(BF16)

    upd = []
    for h in heads:
        kh = k_dec[:, ks[h]]
        k_by_chunk = jnp.concatenate(
            [jnp.where(row_chunk == c, kh, jnp.zeros_like(kh)) for c in chunks], axis=1)
        upd.append(lax.dot_general(v[:, vs[h]], k_by_chunk, (((0,), (0,)), ((), ())),
                                   preferred_element_type=F32))
    scores = [lax.dot_general(q_intra[:, ks[h]], k_dec[:, ks[h]], (((1,), (1,)), ((), ())),
                              preferred_element_type=F32) for h in heads]
    scores = [jnp.where(same_chunk, s, 0.0).astype(BF16) for s in scores]
    o_intra = [jnp.dot(scores[h], v[:, vs[h]], preferred_element_type=F32) for h in heads]
    return dec_tot, q_inter, upd, o_intra


def _gla_carry(local, r, ng, state):
    dec_tot, q_inter, upd, o_intra = local
    heads = range(GLA_HEADS)
    chunks = range(GLA_RB // CHUNK)
    ks = [slice(h * GLA_HEAD_K, (h + 1) * GLA_HEAD_K) for h in heads]
    vs = [slice(h * GLA_HEAD_V, (h + 1) * GLA_HEAD_V) for h in heads]
    rows = [slice(c * CHUNK, (c + 1) * CHUNK) for c in chunks]
    state = list(state)
    o_inter = {}
    for h in heads:
        for c in chunks:
            o_inter[h, c] = lax.dot_general(q_inter[rows[c], ks[h]], state[h].astype(BF16),
                                            (((1,), (1,)), ((), ())),
                                            preferred_element_type=F32)
            state[h] = (state[h] * dec_tot[c * CHUNK:c * CHUNK + 1, ks[h]]
                        + upd[h][:, c * GLA_HEAD_K:(c + 1) * GLA_HEAD_K])
    outs = []
    for h in heads:
        o = o_intra[h] + jnp.concatenate([o_inter[h, c] for c in chunks], axis=0)
        o = o * lax.rsqrt(jnp.mean(o * o, axis=-1, keepdims=True) + RMS_EPS) * ng
        outs.append(_silu(r[:, vs[h]]) * o)
    return outs, state


def _gla_chunk_matrix():
    idx = jnp.arange(GLA_RB)
    same = (idx[:, None] // CHUNK) == (idx[None, :] // CHUNK)
    later = same & (idx[None, :] > idx[:, None])
    top = jnp.concatenate([later, later], axis=1)
    bot = jnp.concatenate([same, same], axis=1)
    return jnp.concatenate([top, bot], axis=0).astype(BF16)


def _sb_kernel(q_ref, k_ref, v_ref, t_ref, o_ref):
    i = pl.program_id(2)
    qb, kb = SB_QB, SB_KB
    tmat = t_ref[...]
    pair_lanes = [slice(p * LANES, (p + 1) * LANES) for p in range(SB_PAIRS)]

    def split_heads(x):
        zero = jnp.zeros_like(x)
        keep = lax.broadcasted_iota(jnp.int32, x.shape, 1) < SB_HEAD
        return jnp.concatenate([jnp.where(keep, x, zero), jnp.where(keep, zero, x)], axis=0)

    def queries(lanes, row0, rows):
        return split_heads(q_ref[0, row0:row0 + rows, lanes] * (SB_HEAD ** -0.5))

    def scores(q2, lanes, start, nkeys):
        kwin = k_ref[0, pl.ds(start, nkeys), lanes]
        return lax.dot_general(q2, kwin, (((1,), (1,)), ((), ())),
                               preferred_element_type=F32)

    def gates(z, visible):
        if visible is not None:
            z = jnp.where(visible, z, SB_MASKED_LOGIT)
        log_beta = jnp.minimum(z, 0.0) - jnp.log(1.0 + jnp.exp2(jnp.abs(z) * -LOG2E))
        log_keep = log_beta - z
        lk_hi = log_keep.astype(BF16)
        lk_lo = (log_keep - lk_hi.astype(F32)).astype(BF16)
        return log_beta, jnp.concatenate([lk_hi, lk_lo], axis=1)

    def block_sums(lk):
        sums = jnp.dot(lk, tmat, preferred_element_type=F32)
        return sums[:, :kb], sums[:, kb:]

    def weights(log_beta, within, run):
        return jnp.exp(log_beta + within + run).astype(BF16)

    def attend(a_blocks, lanes, start):
        rows = a_blocks[0].shape[0] // 2
        vwin = v_ref[0, pl.ds(start, len(a_blocks) * kb), lanes]
        per_head = [jnp.dot(jnp.concatenate([a[hd * rows:(hd + 1) * rows] for a in a_blocks], axis=1),
                            vwin, preferred_element_type=F32) for hd in range(2)]
        keep = lax.broadcasted_iota(jnp.int32, (rows, LANES), 1) < SB_HEAD
        return jnp.where(keep, per_head[0], per_head[1])

    def sweep(lanes, j0, covered, run, acc):
        q2 = queries(lanes, 0, qb)
        s_loc = lax.broadcasted_iota(jnp.int32, (2 * qb, kb), 1)

        def cond(carry):
            j, run, _ = carry
            return jnp.logical_and(j >= 0, jnp.max(run) > SB_LOG_ZERO)

        def body(carry):
            j, run, acc = carry
            start = pl.multiple_of(j * kb, kb)
            fresh = start + s_loc < covered
            log_beta, lk = gates(scores(q2, lanes, start, kb), fresh)
            within, total = block_sums(lk)
            acc = acc + attend([weights(log_beta, within, run)], lanes, start)
            return j - 1, run + total, acc

        return lax.while_loop(cond, body, (j0, run, acc))[2]

    def window(rows, nblk):
        nslab = qb // rows
        units = [(p, h) for p in range(SB_PAIRS) for h in range(nslab)]
        win0 = [pl.multiple_of(jnp.maximum(i * qb + (h + 1) * rows - nblk * kb, 0), rows)
                for h in range(nslab)]
        r_loc = lax.broadcasted_iota(jnp.int32, (2 * rows, kb), 0) % rows
        s_loc = lax.broadcasted_iota(jnp.int32, (2 * rows, kb), 1)
        last_visible = [s_loc < r_loc + (i * qb + h * rows - win0[h] - (nblk - 1) * kb)
                        for h in range(nslab)]

        q2 = {(p, h): queries(pair_lanes[p], h * rows, rows) for p, h in units}
        z = {u: scores(q2[u], pair_lanes[u[0]], win0[u[1]], nblk * kb) for u in units}
        gate = {(u, w): gates(z[u][:, w * kb:(w + 1) * kb],
                              last_visible[u[1]] if w == nblk - 1 else None)
                for u in units for w in range(nblk)}
        sums = {uw: block_sums(gate[uw][1]) for uw in gate}
        run, acc = {}, {}
        for u in units:
            run[u] = jnp.zeros((2 * rows, kb), F32)
            a_blocks = [None] * nblk
            for w in reversed(range(nblk)):
                a_blocks[w] = weights(gate[u, w][0], sums[u, w][0], run[u])
                run[u] = run[u] + sums[u, w][1]
            acc[u] = attend(a_blocks, pair_lanes[u[0]], win0[u[1]])
        left = functools.reduce(jnp.maximum, run.values())
        left = jnp.max(jnp.max(left, axis=0, keepdims=True)[:, :1])

        def done():
            return tuple(jnp.concatenate([acc[p, h] for h in range(nslab)], axis=0)
                         for p in range(SB_PAIRS))

        def rest():
            covered = jnp.concatenate([jnp.full((rows, kb), 1, jnp.int32) * win0[h]
                                       for _ in range(2) for h in range(nslab)], axis=0)
            outs = []
            for p, lanes in enumerate(pair_lanes):
                run_p = jnp.concatenate([run[p, h][hd * rows:(hd + 1) * rows]
                                         for hd in range(2) for h in range(nslab)], axis=0)
                outs.append(sweep(lanes, j0, covered, run_p, done()[p]))
            return tuple(outs)

        j0 = (win0[-1] + kb - 1) // kb - 1
        more = jnp.logical_and(j0 >= 0, left > SB_LOG_ZERO)
        outs = lax.cond(more, rest, done)
        for p, lanes in enumerate(pair_lanes):
            o_ref[0, :, lanes] = outs[p].astype(o_ref.dtype)

    full = i * qb >= (SB_SLAB_BLOCKS - 1) * kb
    pl.when(full)(lambda: window(SB_SLAB, SB_SLAB_BLOCKS))
    pl.when(jnp.logical_not(full))(lambda: window(qb, 1))


def _sb_sum_matrix():
    j = jnp.arange(SB_KB)
    later = j[:, None] > j[None, :]
    half = jnp.concatenate([later, jnp.ones((SB_KB, SB_KB), bool)], axis=1)
    return jnp.concatenate([half, half], axis=0).astype(BF16)


def _sb_attn(q, kv, bsz, seq):
    q3 = q.reshape(bsz, seq, D_MODEL)
    kv3 = kv.reshape(bsz, seq, 2 * D_MODEL)
    width = SB_PAIRS * LANES
    groups = D_MODEL // width
    return pl.pallas_call(
        _sb_kernel,
        grid=(bsz, groups, seq // SB_QB),
        in_specs=[
            pl.BlockSpec((1, SB_QB, width), lambda b, p, i: (b, i, p)),
            pl.BlockSpec((1, seq, width), lambda b, p, i: (b, 0, p)),
            pl.BlockSpec((1, seq, width), lambda b, p, i: (b, 0, groups + p)),
            pl.BlockSpec((2 * SB_KB, 2 * SB_KB), lambda b, p, i: (0, 0)),
        ],
        out_specs=pl.BlockSpec((1, SB_QB, width), lambda b, p, i: (b, i, p)),
        out_shape=jax.ShapeDtypeStruct((bsz, seq, D_MODEL), BF16),
        compiler_params=pltpu.CompilerParams(
            dimension_semantics=("parallel", "parallel", "arbitrary"),
            vmem_limit_bytes=VMEM_LIMIT),
        name="sb_attn",
    )(q3, kv3, kv3, _sb_sum_matrix())


def kernel(x, ln_g, ln_b, ffn_w_up, ffn_w_down, gla_w_in, gla_w_gk, gla_b_gk, gla_norm_g,
           gla_w_out, sb_w_kv, sb_w_q, sb_w_out):
    bsz, seq, d = x.shape
    assert d == D_MODEL and seq % SB_QB == 0 and (bsz * seq) % BLK_TM == 0
    assert SB_QB == SB_KB and SB_QB % SB_SLAB == 0 and DEPTH == 2 and N_A_LAYERS == 1
    x = x.reshape(bsz * seq, d)
    w_up, w_down = ffn_w_up, ffn_w_down
    assert FFN_WU_COLS % FFN_TF == 0 and 2 * D_FF % FFN_W_STEPS == 0 and FFN_WD_ROWS % 16 == 0

    w_in = gla_w_in[0].astype(BF16)
    w_low = jnp.pad(w_in[:, -GATE_RANK:], ((0, 0), (0, LANES - GATE_RANK)))
    w_gk = jnp.pad(gla_w_gk[0], ((0, LANES - GATE_RANK), (0, 0))).astype(BF16)
    x, y = _block(w_up, w_down, 0, 0, ln_g[0, 0], ln_b[0, 0], x=x,
                  post=("gla", w_in, w_low, w_gk, gla_b_gk[0], gla_norm_g[0], seq))
    x, kv = _block(w_up, w_down, 0, 1, ln_g[0, 2], ln_b[0, 2],
                   pre=(y, x, gla_w_out[0].astype(BF16), ln_g[0, 1], ln_b[0, 1]),
                   post=("matmul", sb_w_kv.astype(BF16)))

    x, q = _block(w_up, w_down, 1, 0, ln_g[1, 0], ln_b[1, 0], x=x,
                  post=("matmul", sb_w_q[0].astype(BF16)))
    y = _sb_attn(q, kv, bsz, seq).reshape(bsz * seq, D_MODEL)
    (x,) = _block(w_up, w_down, 1, 1, ln_g[1, 2], ln_b[1, 2],
                  pre=(y, x, sb_w_out[0].astype(BF16), ln_g[1, 1], ln_b[1, 1]))
    return x.reshape(bsz, seq, d)
```

```python
import functools

import jax
import jax.numpy as jnp
from jax import lax
from jax.experimental import pallas as pl
from jax.experimental.pallas import tpu as pltpu

F32 = jnp.float32
BF16 = jnp.bfloat16

D_MODEL = 1024
DEPTH = 2
CHUNK = 64
N_A_LAYERS = DEPTH // 2
D_FF = 2816
GLA_HEADS = 4
GLA_DK = D_MODEL // 2
GLA_DV = D_MODEL
GLA_HEAD_K = GLA_DK // GLA_HEADS
GLA_HEAD_V = GLA_DV // GLA_HEADS
GATE_RANK = 16
GATE_TAU = 16.0
SB_HEADS = 16
SB_HEAD = D_MODEL // SB_HEADS
DEEPNORM_ALPHA = (2 * DEPTH) ** 0.25
LN_EPS = 1e-5
RMS_EPS = 1e-6

LANES = 128
VMEM_LIMIT = 48 * 1024 * 1024
VMEM_TEMP_BYTES = 10 * 1024 * 1024

BLK_TM = 512
BLK_LN_ROWS = 256
FFN_TF = 256
FFN_W_STEPS = 11
FFN_WU_COLS = 2 * D_FF // FFN_W_STEPS
FFN_WD_ROWS = D_FF // FFN_W_STEPS
GLA_RB = 256
SB_QB = 128
SB_STEP_BLOCKS = 4
SB_KB = 128
SB_SLAB = 64
SB_SLAB_BLOCKS = 2
SB_PAIRS = 8
SB_LOG_ZERO = -104.0
SB_MASKED_LOGIT = -1e30
LOG2E = 1.4426950408889634


def _layer_norm(y, g, b):
    mu = jnp.mean(y, axis=-1, keepdims=True)
    yc = y - mu
    var = jnp.mean(yc * yc, axis=-1, keepdims=True)
    return yc * lax.rsqrt(var + LN_EPS) * g + b


def _silu(x):
    return x * jax.nn.sigmoid(x)


def _block_kernel(*refs, pre, post, seq_tiles):
    step = pl.program_id(0)
    refs = list(refs)
    r = {}

    def take(*names):
        for name in names:
            r[name] = refs.pop(0)

    if pre:
        take("y_ref", "xres_ref", "wo_ref", "g1_ref", "b1_ref")
    else:
        take("x_ref")
    take("wu_in_ref", "wd_in_ref", "g2_ref", "b2_ref")
    if post == "gla":
        take("wm_ref", "wl_ref", "wgk_ref", "bgk_ref", "cm_ref", "ng_ref")
    elif post == "matmul":
        take("wp_ref")
    take("o_ref")
    if post:
        take("p_ref")
    take("wu_ref", "wd_ref", "act_ref")
    if pre:
        take("x_ref")
    if post:
        take("ob_ref")
    if post == "gla":
        take("st_ref")
    assert not refs

    @pl.when(step < FFN_W_STEPS)
    def _():
        r["wu_ref"][step] = r["wu_in_ref"][...].astype(BF16)
        rows = pl.ds(pl.multiple_of(step * FFN_WD_ROWS, FFN_WD_ROWS), FFN_WD_ROWS)
        r["wd_ref"][rows, :] = r["wd_in_ref"][...].astype(BF16)

    first_of_sequence = (step - FFN_W_STEPS) % seq_tiles == 0 if post == "gla" else None
    pl.when(step >= FFN_W_STEPS)(functools.partial(
        _block_tile, r, pre=pre, post=post, first_of_sequence=first_of_sequence))


def _w_up_cols(wu_ref, col0):
    off = col0 % FFN_WU_COLS
    return wu_ref[col0 // FFN_WU_COLS, :, off:off + FFN_TF]


def _block_tile(r, *, pre, post, first_of_sequence):
    x_ref, o_ref, act_ref, wu_ref, wd_ref = (r[k] for k in
                                             ("x_ref", "o_ref", "act_ref", "wu_ref", "wd_ref"))
    g2_ref, b2_ref = r["g2_ref"], r["b2_ref"]
    if pre:
        y_ref, xres_ref, wo_ref, g1_ref, b1_ref = (r[k] for k in
                                                   ("y_ref", "xres_ref", "wo_ref", "g1_ref", "b1_ref"))
    if post:
        p_ref, ob_ref = r["p_ref"], r["ob_ref"]
    if post == "matmul":
        wp_ref = r["wp_ref"]
    if post == "gla":
        wm_ref, wl_ref, wgk_ref, bgk_ref, cm_ref, ng_ref, st_ref = (r[k] for k in (
            "wm_ref", "wl_ref", "wgk_ref", "bgk_ref", "cm_ref", "ng_ref", "st_ref"))

        @pl.when(first_of_sequence)
        def _():
            st_ref[...] = jnp.zeros_like(st_ref)

    row_blocks = [slice(blk * BLK_LN_ROWS, (blk + 1) * BLK_LN_ROWS)
                  for blk in range(BLK_TM // BLK_LN_ROWS)]
    if pre:
        for rows in row_blocks:
            mix = jnp.dot(y_ref[rows, :], wo_ref[...], preferred_element_type=F32)
            x_ref[rows, :] = _layer_norm(DEEPNORM_ALPHA * xres_ref[rows, :] + mix,
                                         g1_ref[...], b1_ref[...])

    for rows in (row_blocks if pre else [slice(0, BLK_TM)]):
        xb = x_ref[rows, :].astype(BF16)
        for c in range(D_FF // FFN_TF):
            cols = slice(c * FFN_TF, (c + 1) * FFN_TF)
            gate = jnp.dot(xb, _w_up_cols(wu_ref, c * FFN_TF), preferred_element_type=F32)
            up = jnp.dot(xb, _w_up_cols(wu_ref, D_FF + c * FFN_TF), preferred_element_type=F32)
            act_ref[rows, cols] = (_silu(gate) * up).astype(BF16)

    for rows in row_blocks:
        down = jnp.dot(act_ref[rows, :], wd_ref[...], preferred_element_type=F32)
        out = _layer_norm(DEEPNORM_ALPHA * x_ref[rows, :] + 0.5 * down, g2_ref[...], b2_ref[...])
        o_ref[rows, :] = out
        if post:
            ob_ref[rows, :] = out.astype(BF16)

    if post == "matmul":
        for rows in row_blocks:
            p_ref[rows, :] = jnp.dot(ob_ref[rows, :], wp_ref[...],
                                     preferred_element_type=F32).astype(p_ref.dtype)
    elif post == "gla":
        state = [st_ref[h] for h in range(GLA_HEADS)]
        obs = [ob_ref[rows, :] for rows in row_blocks]
        lows = [jnp.dot(ob, wl_ref[...], preferred_element_type=F32) for ob in obs]
        gate_pre = [jnp.dot(low.astype(BF16), wgk_ref[...], preferred_element_type=F32)
                    + bgk_ref[...] for low in lows]
        projs = [jnp.dot(ob, wm_ref[...], preferred_element_type=F32) for ob in obs]
        local = []
        for proj, pre_g in zip(projs, gate_pre):
            log_g = jax.nn.log_sigmoid(pre_g) / GATE_TAU
            q, k = proj[:, :GLA_DK], proj[:, GLA_DK:2 * GLA_DK]
            v = proj[:, 2 * GLA_DK:2 * GLA_DK + GLA_DV]
            local.append(_gla_local(q, k, v, log_g, cm_ref[...]))
        for rows, proj, loc in zip(row_blocks, projs, local):
            outs, state = _gla_carry(loc, proj[:, 2 * GLA_DK + GLA_DV:], ng_ref[...], state)
            for h, out in enumerate(outs):
                p_ref[rows, h * GLA_HEAD_V:(h + 1) * GLA_HEAD_V] = out.astype(p_ref.dtype)
        for h in range(GLA_HEADS):
            st_ref[h] = state[h]


def _nbytes(shape, dtype):
    n = jnp.dtype(dtype).itemsize
    for s in shape:
        n *= s
    return n


def _block(w_up, w_down, layer, half, g2, b2, *, x=None, pre=None, post=None):
    d = D_MODEL
    m = (pre[1] if pre else x).shape[0]
    resident = pl.Buffered(1)
    vmem = [0]

    def tile(n, dtype):
        vmem[0] += 2 * _nbytes((BLK_TM, n), dtype)
        return pl.BlockSpec((BLK_TM, n), lambda s: (jnp.maximum(s - FFN_W_STEPS, 0), 0))

    def whole(a):
        vmem[0] += _nbytes(a.shape, a.dtype)
        return pl.BlockSpec(a.shape, lambda i: (0,) * a.ndim, pipeline_mode=resident)

    def weight_chunks(a, rows, cols, axis):
        vmem[0] += 2 * _nbytes((rows, cols), a.dtype)
        last = FFN_W_STEPS - 1
        if axis == 0:
            return pl.BlockSpec((None, None, rows, cols),
                                lambda s: (layer, half, jnp.minimum(s, last), 0))
        return pl.BlockSpec((None, None, rows, cols),
                            lambda s: (layer, half, 0, jnp.minimum(s, last)))

    def scratch(n, dtype):
        vmem[0] += _nbytes((BLK_TM, n), dtype)
        return pltpu.VMEM((BLK_TM, n), dtype)

    args, in_specs = [], []
    if pre:
        y, xres, w_out, g1, b1 = pre
        args += [y, xres, w_out, g1.reshape(1, d), b1.reshape(1, d)]
        in_specs += [tile(y.shape[1], y.dtype), tile(d, F32)] + [whole(a) for a in args[2:]]
    else:
        args += [x]
        in_specs += [tile(d, F32)]
    vecs = [g2.reshape(1, d), b2.reshape(1, d)]
    args += [w_up, w_down] + vecs
    in_specs += [weight_chunks(w_up, d, FFN_WU_COLS, 1), weight_chunks(w_down, FFN_WD_ROWS, d, 0)]
    in_specs += [whole(a) for a in vecs]
    out_shape = [jax.ShapeDtypeStruct((m, d), F32)]
    out_specs = [tile(d, F32)]
    kind = post[0] if post else None
    seq_tiles = None
    if kind == "matmul":
        w_p = post[1]
        args += [w_p]
        in_specs += [whole(w_p)]
        out_shape += [jax.ShapeDtypeStruct((m, w_p.shape[1]), BF16)]
        out_specs += [tile(w_p.shape[1], BF16)]
    elif kind == "gla":
        w_in, w_low, w_gk, b_gk, norm_g, seq = post[1:]
        assert BLK_LN_ROWS == GLA_RB and seq % BLK_TM == 0
        seq_tiles = seq // BLK_TM
        n_main = 2 * GLA_DK + 2 * GLA_DV
        extra = [w_low, w_gk, b_gk.reshape(1, GLA_DK), _gla_chunk_matrix(),
                 norm_g.reshape(1, GLA_HEAD_V)]
        args += [w_in] + extra
        vmem[0] += _nbytes((d, n_main), w_in.dtype)
        in_specs += [pl.BlockSpec((d, n_main), lambda i: (0, 0), pipeline_mode=resident)]
        in_specs += [whole(a) for a in extra]
        out_shape += [jax.ShapeDtypeStruct((m, GLA_DV), BF16)]
        out_specs += [tile(GLA_DV, BF16)]
    vmem[0] += _nbytes((d, 2 * D_FF), BF16) + _nbytes((D_FF, d), BF16)
    scratch_shapes = [pltpu.VMEM((FFN_W_STEPS, d, FFN_WU_COLS), BF16), pltpu.VMEM((D_FF, d), BF16),
                      scratch(D_FF, BF16)]
    if pre:
        scratch_shapes += [scratch(d, F32)]
    if post:
        scratch_shapes += [scratch(d, BF16)]
    if kind == "gla":
        state_shape = (GLA_HEADS, GLA_HEAD_V, GLA_HEAD_K)
        vmem[0] += _nbytes(state_shape, F32)
        scratch_shapes += [pltpu.VMEM(state_shape, F32)]
    return pl.pallas_call(
        functools.partial(_block_kernel, pre=bool(pre), post=kind, seq_tiles=seq_tiles),
        grid=(FFN_W_STEPS + m // BLK_TM,),
        in_specs=in_specs,
        out_specs=out_specs,
        out_shape=out_shape,
        scratch_shapes=scratch_shapes,
        compiler_params=pltpu.CompilerParams(
            dimension_semantics=("arbitrary",),
            vmem_limit_bytes=vmem[0] + VMEM_TEMP_BYTES),
        name="block" + ("_pre" if pre else "") + ("_" + kind if kind else ""),
    )(*args)


def _gla_local(q, k, v, lg, chunk_matrix):
    rb = GLA_RB
    heads = range(GLA_HEADS)
    chunks = range(rb // CHUNK)
    ks = [slice(h * GLA_HEAD_K, (h + 1) * GLA_HEAD_K) for h in heads]
    vs = [slice(h * GLA_HEAD_V, (h + 1) * GLA_HEAD_V) for h in heads]
    same_chunk = (lax.broadcasted_iota(jnp.int32, (rb, rb), 0) // CHUNK
                  == lax.broadcasted_iota(jnp.int32, (rb, rb), 1) // CHUNK)
    row_chunk = lax.broadcasted_iota(jnp.int32, (rb, GLA_HEAD_K), 0) // CHUNK

    lg_hi = lg.astype(BF16)
    lg_lo = (lg - lg_hi.astype(F32)).astype(BF16)
    sums = jnp.dot(chunk_matrix, jnp.concatenate([lg_hi, lg_lo], axis=0),
                   preferred_element_type=F32)
    e_end = sums[:rb]
    dec_tot = jnp.exp(sums[rb:])
    q = q * (GLA_HEAD_K ** -0.5)
    k_dec = (k * jnp.exp(e_end)).astype(BF16)
    q_intra = q.astype(BF16)
    q_inter = (q * dec_tot).astype(BF16)
    v = v.astype(BF16)

    upd = []
    for h in heads:
        kh = k_dec[:, ks[h]]
        k_by_chunk = jnp.concatenate(
            [jnp.where(row_chunk == c, kh, jnp.zeros_like(kh)) for c in chunks], axis=1)
        upd.append(lax.dot_general(v[:, vs[h]], k_by_chunk, (((0,), (0,)), ((), ())),
                                   preferred_element_type=F32))
    scores = [lax.dot_general(q_intra[:, ks[h]], k_dec[:, ks[h]], (((1,), (1,)), ((), ())),
                              preferred_element_type=F32) for h in heads]
    scores = [jnp.where(same_chunk, s, 0.0).astype(BF16) for s in scores]
    o_intra = [jnp.dot(scores[h], v[:, vs[h]], preferred_element_type=F32) for h in heads]
    return dec_tot, q_inter, upd, o_intra


def _gla_carry(local, r, ng, state):
    dec_tot, q_inter, upd, o_intra = local
    heads = range(GLA_HEADS)
    chunks = range(GLA_RB // CHUNK)
    ks = [slice(h * GLA_HEAD_K, (h + 1) * GLA_HEAD_K) for h in heads]
    vs = [slice(h * GLA_HEAD_V, (h + 1) * GLA_HEAD_V) for h in heads]
    rows = [slice(c * CHUNK, (c + 1) * CHUNK) for c in chunks]
    state = list(state)
    o_inter = {}
    for h in heads:
        for c in chunks:
            o_inter[h, c] = lax.dot_general(q_inter[rows[c], ks[h]], state[h].astype(BF16),
                                            (((1,), (1,)), ((), ())),
                                            preferred_element_type=F32)
            state[h] = (state[h] * dec_tot[c * CHUNK:c * CHUNK + 1, ks[h]]
                        + upd[h][:, c * GLA_HEAD_K:(c + 1) * GLA_HEAD_K])
    outs = []
    for h in heads:
        o = o_intra[h] + jnp.concatenate([o_inter[h, c] for c in chunks], axis=0)
        o = o * lax.rsqrt(jnp.mean(o * o, axis=-1, keepdims=True) + RMS_EPS) * ng
        outs.append(_silu(r[:, vs[h]]) * o)
    return outs, state


def _gla_chunk_matrix():
    idx = jnp.arange(GLA_RB)
    same = (idx[:, None] // CHUNK) == (idx[None, :] // CHUNK)
    later = same & (idx[None, :] > idx[:, None])
    top = jnp.concatenate([later, later], axis=1)
    bot = jnp.concatenate([same, same], axis=1)
    return jnp.concatenate([top, bot], axis=0).astype(BF16)


def _sb_kernel(q_ref, k_ref, v_ref, t_ref, o_ref):
    for blk in range(SB_STEP_BLOCKS):
        rows = slice(blk * SB_QB, (blk + 1) * SB_QB)
        _sb_query_block(pl.program_id(2) * SB_STEP_BLOCKS + blk, q_ref.at[:, rows, :],
                        k_ref, v_ref, t_ref, o_ref.at[:, rows, :])


def _sb_query_block(i, q_ref, k_ref, v_ref, t_ref, o_ref):
    qb, kb = SB_QB, SB_KB
    tmat = t_ref[...]
    pair_lanes = [slice(p * LANES, (p + 1) * LANES) for p in range(SB_PAIRS)]

    def split_heads(x):
        zero = jnp.zeros_like(x)
        keep = lax.broadcasted_iota(jnp.int32, x.shape, 1) < SB_HEAD
        return jnp.concatenate([jnp.where(keep, x, zero), jnp.where(keep, zero, x)], axis=0)

    def queries(lanes, row0, rows):
        return split_heads(q_ref[0, row0:row0 + rows, lanes] * (SB_HEAD ** -0.5))

    def scores(q2, lanes, start, nkeys):
        kwin = k_ref[0, pl.ds(start, nkeys), lanes]
        return lax.dot_general(q2, kwin, (((1,), (1,)), ((), ())),
                               preferred_element_type=F32)

    def gates(z, visible):
        if visible is not None:
            z = jnp.where(visible, z, SB_MASKED_LOGIT)
        log_beta = jnp.minimum(z, 0.0) - jnp.log(1.0 + jnp.exp2(jnp.abs(z) * -LOG2E))
        log_keep = log_beta - z
        lk_hi = log_keep.astype(BF16)
        lk_lo = (log_keep - lk_hi.astype(F32)).astype(BF16)
        return log_beta, jnp.concatenate([lk_hi, lk_lo], axis=1)

    def block_sums(lk):
        sums = jnp.dot(lk, tmat, preferred_element_type=F32)
        return sums[:, :kb], sums[:, kb:]

    def weights(log_beta, within, run):
        return jnp.exp(log_beta + within + run).astype(BF16)

    def attend(a_blocks, lanes, start):
        rows = a_blocks[0].shape[0] // 2
        vwin = v_ref[0, pl.ds(start, len(a_blocks) * kb), lanes]
        per_head = [jnp.dot(jnp.concatenate([a[hd * rows:(hd + 1) * rows] for a in a_blocks], axis=1),
                            vwin, preferred_element_type=F32) for hd in range(2)]
        keep = lax.broadcasted_iota(jnp.int32, (rows, LANES), 1) < SB_HEAD
        return jnp.where(keep, per_head[0], per_head[1])

    def sweep(lanes, j0, covered, run, acc):
        q2 = queries(lanes, 0, qb)
        s_loc = lax.broadcasted_iota(jnp.int32, (2 * qb, kb), 1)

        def cond(carry):
            j, run, _ = carry
            return jnp.logical_and(j >= 0, jnp.max(run) > SB_LOG_ZERO)

        def body(carry):
            j, run, acc = carry
            start = pl.multiple_of(j * kb, kb)
            fresh = start + s_loc < covered
            log_beta, lk = gates(scores(q2, lanes, start, kb), fresh)
            within, total = block_sums(lk)
            acc = acc + attend([weights(log_beta, within, run)], lanes, start)
            return j - 1, run + total, acc

        return lax.while_loop(cond, body, (j0, run, acc))[2]

    def window(rows, nblk):
        nslab = qb // rows
        units = [(p, h) for p in range(SB_PAIRS) for h in range(nslab)]
        win0 = [pl.multiple_of(jnp.maximum(i * qb + (h + 1) * rows - nblk * kb, 0), rows)
                for h in range(nslab)]
        r_loc = lax.broadcasted_iota(jnp.int32, (2 * rows, kb), 0) % rows
        s_loc = lax.broadcasted_iota(jnp.int32, (2 * rows, kb), 1)
        last_visible = [s_loc < r_loc + (i * qb + h * rows - win0[h] - (nblk - 1) * kb)
                        for h in range(nslab)]

        q2 = {(p, h): queries(pair_lanes[p], h * rows, rows) for p, h in units}
        z = {u: scores(q2[u], pair_lanes[u[0]], win0[u[1]], nblk * kb) for u in units}
        gate = {(u, w): gates(z[u][:, w * kb:(w + 1) * kb],
                              last_visible[u[1]] if w == nblk - 1 else None)
                for u in units for w in range(nblk)}
        sums = {uw: block_sums(gate[uw][1]) for uw in gate}
        run, acc = {}, {}
        for u in units:
            run[u] = jnp.zeros((2 * rows, kb), F32)
            a_blocks = [None] * nblk
            for w in reversed(range(nblk)):
                a_blocks[w] = weights(gate[u, w][0], sums[u, w][0], run[u])
                run[u] = run[u] + sums[u, w][1]
            acc[u] = attend(a_blocks, pair_lanes[u[0]], win0[u[1]])
        left = functools.reduce(jnp.maximum, run.values())
        left = jnp.max(jnp.max(left, axis=0, keepdims=True)[:, :1])

        def done():
            return tuple(jnp.concatenate([acc[p, h] for h in range(nslab)], axis=0)
                         for p in range(SB_PAIRS))

        def rest():
            covered = jnp.concatenate([jnp.full((rows, kb), 1, jnp.int32) * win0[h]
                                       for _ in range(2) for h in range(nslab)], axis=0)
            outs = []
            for p, lanes in enumerate(pair_lanes):
                run_p = jnp.concatenate([run[p, h][hd * rows:(hd + 1) * rows]
                                         for hd in range(2) for h in range(nslab)], axis=0)
                outs.append(sweep(lanes, j0, covered, run_p, done()[p]))
            return tuple(outs)

        j0 = (win0[-1] + kb - 1) // kb - 1
        more = jnp.logical_and(j0 >= 0, left > SB_LOG_ZERO)
        outs = lax.cond(more, rest, done)
        for p, lanes in enumerate(pair_lanes):
            o_ref[0, :, lanes] = outs[p].astype(o_ref.dtype)

    full = i * qb >= (SB_SLAB_BLOCKS - 1) * kb
    pl.when(full)(lambda: window(SB_SLAB, SB_SLAB_BLOCKS))
    pl.when(jnp.logical_not(full))(lambda: window(qb, 1))


def _sb_sum_matrix():
    j = jnp.arange(SB_KB)
    later = j[:, None] > j[None, :]
    half = jnp.concatenate([later, jnp.ones((SB_KB, SB_KB), bool)], axis=1)
    return jnp.concatenate([half, half], axis=0).astype(BF16)


def _sb_attn(q, kv, bsz, seq):
    q3 = q.reshape(bsz, seq, D_MODEL)
    kv3 = kv.reshape(bsz, seq, 2 * D_MODEL)
    width = SB_PAIRS * LANES
    groups = D_MODEL // width
    step_rows = SB_STEP_BLOCKS * SB_QB
    return pl.pallas_call(
        _sb_kernel,
        grid=(bsz, groups, seq // step_rows),
        in_specs=[
            pl.BlockSpec((1, step_rows, width), lambda b, p, i: (b, i, p)),
            pl.BlockSpec((1, seq, width), lambda b, p, i: (b, 0, p)),
            pl.BlockSpec((1, seq, width), lambda b, p, i: (b, 0, groups + p)),
            pl.BlockSpec((2 * SB_KB, 2 * SB_KB), lambda b, p, i: (0, 0)),
        ],
        out_specs=pl.BlockSpec((1, step_rows, width), lambda b, p, i: (b, i, p)),
        out_shape=jax.ShapeDtypeStruct((bsz, seq, D_MODEL), BF16),
        compiler_params=pltpu.CompilerParams(
            dimension_semantics=("parallel", "parallel", "arbitrary"),
            vmem_limit_bytes=VMEM_LIMIT),
        name="sb_attn",
    )(q3, kv3, kv3, _sb_sum_matrix())


def kernel(x, ln_g, ln_b, ffn_w_up, ffn_w_down, gla_w_in, gla_w_gk, gla_b_gk, gla_norm_g,
           gla_w_out, sb_w_kv, sb_w_q, sb_w_out):
    bsz, seq, d = x.shape
    assert d == D_MODEL and seq % (SB_STEP_BLOCKS * SB_QB) == 0 and (bsz * seq) % BLK_TM == 0
    assert SB_QB == SB_KB and SB_QB % SB_SLAB == 0 and DEPTH == 2 and N_A_LAYERS == 1
    x = x.reshape(bsz * seq, d)
    w_up, w_down = ffn_w_up, ffn_w_down
    assert FFN_WU_COLS % FFN_TF == 0 and 2 * D_FF % FFN_W_STEPS == 0 and FFN_WD_ROWS % 16 == 0

    w_in = gla_w_in[0].astype(BF16)
    w_low = jnp.pad(w_in[:, -GATE_RANK:], ((0, 0), (0, LANES - GATE_RANK)))
    w_gk = jnp.pad(gla_w_gk[0], ((0, LANES - GATE_RANK), (0, 0))).astype(BF16)
    x, y = _block(w_up, w_down, 0, 0, ln_g[0, 0], ln_b[0, 0], x=x,
                  post=("gla", w_in, w_low, w_gk, gla_b_gk[0], gla_norm_g[0], seq))
    x, kv = _block(w_up, w_down, 0, 1, ln_g[0, 2], ln_b[0, 2],
                   pre=(y, x, gla_w_out[0].astype(BF16), ln_g[0, 1], ln_b[0, 1]),
                   post=("matmul", sb_w_kv.astype(BF16)))

    x, q = _block(w_up, w_down, 1, 0, ln_g[1, 0], ln_b[1, 0], x=x,
                  post=("matmul", sb_w_q[0].astype(BF16)))
    y = _sb_attn(q, kv, bsz, seq).reshape(bsz * seq, D_MODEL)
    (x,) = _block(w_up, w_down, 1, 1, ln_g[1, 2], ln_b[1, 2],
                  pre=(y, x, sb_w_out[0].astype(BF16), ln_g[1, 1], ln_b[1, 1]))
    return x.reshape(bsz, seq, d)
```

```python
import functools

import jax
import jax.numpy as jnp
from jax import lax
from jax.experimental import pallas as pl
from jax.experimental.pallas import tpu as pltpu

F32 = jnp.float32
BF16 = jnp.bfloat16

D_MODEL = 1024
DEPTH = 2
CHUNK = 64
N_A_LAYERS = DEPTH // 2
D_FF = 2816
GLA_HEADS = 4
GLA_DK = D_MODEL // 2
GLA_DV = D_MODEL
GLA_HEAD_K = GLA_DK // GLA_HEADS
GLA_HEAD_V = GLA_DV // GLA_HEADS
GATE_RANK = 16
GATE_TAU = 16.0
SB_HEADS = 16
SB_HEAD = D_MODEL // SB_HEADS
DEEPNORM_ALPHA = (2 * DEPTH) ** 0.25
LN_EPS = 1e-5
RMS_EPS = 1e-6

LANES = 128
VMEM_LIMIT = 48 * 1024 * 1024
VMEM_TEMP_BYTES = 10 * 1024 * 1024

BLK_TM = 512
BLK_LN_ROWS = 256
FFN_TF = 256
FFN_W_STEPS = 11
FFN_WU_COLS = 2 * D_FF // FFN_W_STEPS
FFN_WD_ROWS = D_FF // FFN_W_STEPS
GLA_RB = 256
SB_QB = 128
SB_STEP_BLOCKS = 2
SB_KB = 128
SB_SLAB = 64
SB_SLAB_BLOCKS = 2
SB_PAIRS = 8
SB_LOG_ZERO = -104.0
SB_MASKED_LOGIT = -1e30
LOG2E = 1.4426950408889634


def _layer_norm(y, g, b):
    mu = jnp.mean(y, axis=-1, keepdims=True)
    yc = y - mu
    var = jnp.mean(yc * yc, axis=-1, keepdims=True)
    return yc * lax.rsqrt(var + LN_EPS) * g + b


def _silu(x):
    return x * jax.nn.sigmoid(x)


def _block_kernel(*refs, pre, post, seq_tiles):
    step = pl.program_id(0)
    refs = list(refs)
    r = {}

    def take(*names):
        for name in names:
            r[name] = refs.pop(0)

    if pre:
        take("y_ref", "xres_ref", "wo_ref", "g1_ref", "b1_ref")
    else:
        take("x_ref")
    take("wu_in_ref", "wd_in_ref", "g2_ref", "b2_ref")
    if post == "gla":
        take("wm_ref", "wl_ref", "wgk_ref", "bgk_ref", "cm_ref", "ng_ref")
    elif post == "matmul":
        take("wp_ref")
    take("o_ref")
    if post:
        take("p_ref")
    take("wu_ref", "wd_ref", "act_ref")
    if pre:
        take("x_ref")
    if post:
        take("ob_ref")
    if post == "gla":
        take("st_ref")
    assert not refs

    @pl.when(step < FFN_W_STEPS)
    def _():
        r["wu_ref"][step] = r["wu_in_ref"][...].astype(BF16)
        rows = pl.ds(pl.multiple_of(step * FFN_WD_ROWS, FFN_WD_ROWS), FFN_WD_ROWS)
        r["wd_ref"][rows, :] = r["wd_in_ref"][...].astype(BF16)

    first_of_sequence = (step - FFN_W_STEPS) % seq_tiles == 0 if post == "gla" else None
    pl.when(step >= FFN_W_STEPS)(functools.partial(
        _block_tile, r, pre=pre, post=post, first_of_sequence=first_of_sequence))


def _w_up_cols(wu_ref, col0):
    off = col0 % FFN_WU_COLS
    return wu_ref[col0 // FFN_WU_COLS, :, off:off + FFN_TF]


def _block_tile(r, *, pre, post, first_of_sequence):
    x_ref, o_ref, act_ref, wu_ref, wd_ref = (r[k] for k in
                                             ("x_ref", "o_ref", "act_ref", "wu_ref", "wd_ref"))
    g2_ref, b2_ref = r["g2_ref"], r["b2_ref"]
    if pre:
        y_ref, xres_ref, wo_ref, g1_ref, b1_ref = (r[k] for k in
                                                   ("y_ref", "xres_ref", "wo_ref", "g1_ref", "b1_ref"))
    if post:
        p_ref, ob_ref = r["p_ref"], r["ob_ref"]
    if post == "matmul":
        wp_ref = r["wp_ref"]
    if post == "gla":
        wm_ref, wl_ref, wgk_ref, bgk_ref, cm_ref, ng_ref, st_ref = (r[k] for k in (
            "wm_ref", "wl_ref", "wgk_ref", "bgk_ref", "cm_ref", "ng_ref", "st_ref"))

        @pl.when(first_of_sequence)
        def _():
            st_ref[...] = jnp.zeros_like(st_ref)

    row_blocks = [slice(blk * BLK_LN_ROWS, (blk + 1) * BLK_LN_ROWS)
                  for blk in range(BLK_TM // BLK_LN_ROWS)]
    if pre:
        for rows in row_blocks:
            mix = jnp.dot(y_ref[rows, :], wo_ref[...], preferred_element_type=F32)
            x_ref[rows, :] = _layer_norm(DEEPNORM_ALPHA * xres_ref[rows, :] + mix,
                                         g1_ref[...], b1_ref[...])

    for rows in (row_blocks if pre else [slice(0, BLK_TM)]):
        xb = x_ref[rows, :].astype(BF16)
        for c in range(D_FF // FFN_TF):
            cols = slice(c * FFN_TF, (c + 1) * FFN_TF)
            gate = jnp.dot(xb, _w_up_cols(wu_ref, c * FFN_TF), preferred_element_type=F32)
            up = jnp.dot(xb, _w_up_cols(wu_ref, D_FF + c * FFN_TF), preferred_element_type=F32)
            act_ref[rows, cols] = (_silu(gate) * up).astype(BF16)

    for rows in row_blocks:
        down = jnp.dot(act_ref[rows, :], wd_ref[...], preferred_element_type=F32)
        out = _layer_norm(DEEPNORM_ALPHA * x_ref[rows, :] + 0.5 * down, g2_ref[...], b2_ref[...])
        o_ref[rows, :] = out
        if post:
            ob_ref[rows, :] = out.astype(BF16)

    if post == "matmul":
        for rows in row_blocks:
            p_ref[rows, :] = jnp.dot(ob_ref[rows, :], wp_ref[...],
                                     preferred_element_type=F32).astype(p_ref.dtype)
    elif post == "gla":
        state = [st_ref[h] for h in range(GLA_HEADS)]
        obs = [ob_ref[rows, :] for rows in row_blocks]
        lows = [jnp.dot(ob, wl_ref[...], preferred_element_type=F32) for ob in obs]
        gate_pre = [jnp.dot(low.astype(BF16), wgk_ref[...], preferred_element_type=F32)
                    + bgk_ref[...] for low in lows]
        projs = [jnp.dot(ob, wm_ref[...], preferred_element_type=F32) for ob in obs]
        local = []
        for proj, pre_g in zip(projs, gate_pre):
            log_g = jax.nn.log_sigmoid(pre_g) / GATE_TAU
            q, k = proj[:, :GLA_DK], proj[:, GLA_DK:2 * GLA_DK]
            v = proj[:, 2 * GLA_DK:2 * GLA_DK + GLA_DV]
            local.append(_gla_local(q, k, v, log_g, cm_ref[...]))
        for rows, proj, loc in zip(row_blocks, projs, local):
            outs, state = _gla_carry(loc, proj[:, 2 * GLA_DK + GLA_DV:], ng_ref[...], state)
            for h, out in enumerate(outs):
                p_ref[rows, h * GLA_HEAD_V:(h + 1) * GLA_HEAD_V] = out.astype(p_ref.dtype)
        for h in range(GLA_HEADS):
            st_ref[h] = state[h]


def _nbytes(shape, dtype):
    n = jnp.dtype(dtype).itemsize
    for s in shape:
        n *= s
    return n


def _block(w_up, w_down, layer, half, g2, b2, *, x=None, pre=None, post=None):
    d = D_MODEL
    m = (pre[1] if pre else x).shape[0]
    resident = pl.Buffered(1)
    vmem = [0]

    def tile(n, dtype):
        vmem[0] += 2 * _nbytes((BLK_TM, n), dtype)
        return pl.BlockSpec((BLK_TM, n), lambda s: (jnp.maximum(s - FFN_W_STEPS, 0), 0))

    def whole(a):
        vmem[0] += _nbytes(a.shape, a.dtype)
        return pl.BlockSpec(a.shape, lambda i: (0,) * a.ndim, pipeline_mode=resident)

    def weight_chunks(a, rows, cols, axis):
        vmem[0] += 2 * _nbytes((rows, cols), a.dtype)
        last = FFN_W_STEPS - 1
        if axis == 0:
            return pl.BlockSpec((None, None, rows, cols),
                                lambda s: (layer, half, jnp.minimum(s, last), 0))
        return pl.BlockSpec((None, None, rows, cols),
                            lambda s: (layer, half, 0, jnp.minimum(s, last)))

    def scratch(n, dtype):
        vmem[0] += _nbytes((BLK_TM, n), dtype)
        return pltpu.VMEM((BLK_TM, n), dtype)

    args, in_specs = [], []
    if pre:
        y, xres, w_out, g1, b1 = pre
        args += [y, xres, w_out, g1.reshape(1, d), b1.reshape(1, d)]
        in_specs += [tile(y.shape[1], y.dtype), tile(d, F32)] + [whole(a) for a in args[2:]]
    else:
        args += [x]
        in_specs += [tile(d, F32)]
    vecs = [g2.reshape(1, d), b2.reshape(1, d)]
    args += [w_up, w_down] + vecs
    in_specs += [weight_chunks(w_up, d, FFN_WU_COLS, 1), weight_chunks(w_down, FFN_WD_ROWS, d, 0)]
    in_specs += [whole(a) for a in vecs]
    out_shape = [jax.ShapeDtypeStruct((m, d), F32)]
    out_specs = [tile(d, F32)]
    kind = post[0] if post else None
    seq_tiles = None
    if kind == "matmul":
        w_p = post[1]
        args += [w_p]
        in_specs += [whole(w_p)]
        out_shape += [jax.ShapeDtypeStruct((m, w_p.shape[1]), BF16)]
        out_specs += [tile(w_p.shape[1], BF16)]
    elif kind == "gla":
        w_in, w_low, w_gk, b_gk, norm_g, seq = post[1:]
        assert BLK_LN_ROWS == GLA_RB and seq % BLK_TM == 0
        seq_tiles = seq // BLK_TM
        n_main = 2 * GLA_DK + 2 * GLA_DV
        extra = [w_low, w_gk, b_gk.reshape(1, GLA_DK), _gla_chunk_matrix(),
                 norm_g.reshape(1, GLA_HEAD_V)]
        args += [w_in] + extra
        vmem[0] += _nbytes((d, n_main), w_in.dtype)
        in_specs += [pl.BlockSpec((d, n_main), lambda i: (0, 0), pipeline_mode=resident)]
        in_specs += [whole(a) for a in extra]
        out_shape += [jax.ShapeDtypeStruct((m, GLA_DV), BF16)]
        out_specs += [tile(GLA_DV, BF16)]
    vmem[0] += _nbytes((d, 2 * D_FF), BF16) + _nbytes((D_FF, d), BF16)
    scratch_shapes = [pltpu.VMEM((FFN_W_STEPS, d, FFN_WU_COLS), BF16), pltpu.VMEM((D_FF, d), BF16),
                      scratch(D_FF, BF16)]
    if pre:
        scratch_shapes += [scratch(d, F32)]
    if post:
        scratch_shapes += [scratch(d, BF16)]
    if kind == "gla":
        state_shape = (GLA_HEADS, GLA_HEAD_V, GLA_HEAD_K)
        vmem[0] += _nbytes(state_shape, F32)
        scratch_shapes += [pltpu.VMEM(state_shape, F32)]
    return pl.pallas_call(
        functools.partial(_block_kernel, pre=bool(pre), post=kind, seq_tiles=seq_tiles),
        grid=(FFN_W_STEPS + m // BLK_TM,),
        in_specs=in_specs,
        out_specs=out_specs,
        out_shape=out_shape,
        scratch_shapes=scratch_shapes,
        compiler_params=pltpu.CompilerParams(
            dimension_semantics=("arbitrary",),
            vmem_limit_bytes=vmem[0] + VMEM_TEMP_BYTES),
        name="block" + ("_pre" if pre else "") + ("_" + kind if kind else ""),
    )(*args)


def _gla_local(q, k, v, lg, chunk_matrix):
    rb = GLA_RB
    heads = range(GLA_HEADS)
    chunks = range(rb // CHUNK)
    ks = [slice(h * GLA_HEAD_K, (h + 1) * GLA_HEAD_K) for h in heads]
    vs = [slice(h * GLA_HEAD_V, (h + 1) * GLA_HEAD_V) for h in heads]
    same_chunk = (lax.broadcasted_iota(jnp.int32, (rb, rb), 0) // CHUNK
                  == lax.broadcasted_iota(jnp.int32, (rb, rb), 1) // CHUNK)
    row_chunk = lax.broadcasted_iota(jnp.int32, (rb, GLA_HEAD_K), 0) // CHUNK

    lg_hi = lg.astype(BF16)
    lg_lo = (lg - lg_hi.astype(F32)).astype(BF16)
    sums = jnp.dot(chunk_matrix, jnp.concatenate([lg_hi, lg_lo], axis=0),
                   preferred_element_type=F32)
    e_end = sums[:rb]
    dec_tot = jnp.exp(sums[rb:])
    q = q * (GLA_HEAD_K ** -0.5)
    k_dec = (k * jnp.exp(e_end)).astype(BF16)
    q_intra = q.astype(BF16)
    q_inter = (q * dec_tot).astype(BF16)
    v = v.astype(BF16)

    upd = []
    for h in heads:
        kh = k_dec[:, ks[h]]
        k_by_chunk = jnp.concatenate(
            [jnp.where(row_chunk == c, kh, jnp.zeros_like(kh)) for c in chunks], axis=1)
        upd.append(lax.dot_general(v[:, vs[h]], k_by_chunk, (((0,), (0,)), ((), ())),
                                   preferred_element_type=F32))
    scores = [lax.dot_general(q_intra[:, ks[h]], k_dec[:, ks[h]], (((1,), (1,)), ((), ())),
                              preferred_element_type=F32) for h in heads]
    scores = [jnp.where(same_chunk, s, 0.0).astype(BF16) for s in scores]
    o_intra = [jnp.dot(scores[h], v[:, vs[h]], preferred_element_type=F32) for h in heads]
    return dec_tot, q_inter, upd, o_intra


def _gla_carry(local, r, ng, state):
    dec_tot, q_inter, upd, o_intra = local
    heads = range(GLA_HEADS)
    chunks = range(GLA_RB // CHUNK)
    ks = [slice(h * GLA_HEAD_K, (h + 1) * GLA_HEAD_K) for h in heads]
    vs = [slice(h * GLA_HEAD_V, (h + 1) * GLA_HEAD_V) for h in heads]
    rows = [slice(c * CHUNK, (c + 1) * CHUNK) for c in chunks]
    state = list(state)
    o_inter = {}
    for h in heads:
        for c in chunks:
            o_inter[h, c] = lax.dot_general(q_inter[rows[c], ks[h]], state[h].astype(BF16),
                                            (((1,), (1,)), ((), ())),
                                            preferred_element_type=F32)
            state[h] = (state[h] * dec_tot[c * CHUNK:c * CHUNK + 1, ks[h]]
                        + upd[h][:, c * GLA_HEAD_K:(c + 1) * GLA_HEAD_K])
    outs = []
    for h in heads:
        o = o_intra[h] + jnp.concatenate([o_inter[h, c] for c in chunks], axis=0)
        o = o * lax.rsqrt(jnp.mean(o * o, axis=-1, keepdims=True) + RMS_EPS) * ng
        outs.append(_silu(r[:, vs[h]]) * o)
    return outs, state


def _gla_chunk_matrix():
    idx = jnp.arange(GLA_RB)
    same = (idx[:, None] // CHUNK) == (idx[None, :] // CHUNK)
    later = same & (idx[None, :] > idx[:, None])
    top = jnp.concatenate([later, later], axis=1)
    bot = jnp.concatenate([same, same], axis=1)
    return jnp.concatenate([top, bot], axis=0).astype(BF16)


def _sb_kernel(q_ref, k_ref, v_ref, t_ref, o_ref):
    for blk in range(SB_STEP_BLOCKS):
        rows = slice(blk * SB_QB, (blk + 1) * SB_QB)
        _sb_query_block(pl.program_id(2) * SB_STEP_BLOCKS + blk, q_ref.at[:, rows, :],
                        k_ref, v_ref, t_ref, o_ref.at[:, rows, :])


def _sb_query_block(i, q_ref, k_ref, v_ref, t_ref, o_ref):
    qb, kb = SB_QB, SB_KB
    tmat = t_ref[...]
    pair_lanes = [slice(p * LANES, (p + 1) * LANES) for p in range(SB_PAIRS)]

    def split_heads(x):
        zero = jnp.zeros_like(x)
        keep = lax.broadcasted_iota(jnp.int32, x.shape, 1) < SB_HEAD
        return jnp.concatenate([jnp.where(keep, x, zero), jnp.where(keep, zero, x)], axis=0)

    def queries(lanes, row0, rows):
        return split_heads(q_ref[0, row0:row0 + rows, lanes] * (SB_HEAD ** -0.5))

    def scores(q2, lanes, start, nkeys):
        kwin = k_ref[0, pl.ds(start, nkeys), lanes]
        return lax.dot_general(q2, kwin, (((1,), (1,)), ((), ())),
                               preferred_element_type=F32)

    def gates(z, visible):
        if visible is not None:
            z = jnp.where(visible, z, SB_MASKED_LOGIT)
        log_beta = jnp.minimum(z, 0.0) - jnp.log(1.0 + jnp.exp2(jnp.abs(z) * -LOG2E))
        log_keep = log_beta - z
        lk_hi = log_keep.astype(BF16)
        lk_lo = (log_keep - lk_hi.astype(F32)).astype(BF16)
        return log_beta, jnp.concatenate([lk_hi, lk_lo], axis=1)

    def block_sums(lk):
        sums = jnp.dot(lk, tmat, preferred_element_type=F32)
        return sums[:, :kb], sums[:, kb:]

    def weights(log_beta, within, run):
        return jnp.exp(log_beta + within + run).astype(BF16)

    def attend(a_blocks, lanes, start):
        rows = a_blocks[0].shape[0] // 2
        vwin = v_ref[0, pl.ds(start, len(a_blocks) * kb), lanes]
        per_head = [jnp.dot(jnp.concatenate([a[hd * rows:(hd + 1) * rows] for a in a_blocks], axis=1),
                            vwin, preferred_element_type=F32) for hd in range(2)]
        keep = lax.broadcasted_iota(jnp.int32, (rows, LANES), 1) < SB_HEAD
        return jnp.where(keep, per_head[0], per_head[1])

    def sweep(lanes, j0, covered, run, acc):
        q2 = queries(lanes, 0, qb)
        s_loc = lax.broadcasted_iota(jnp.int32, (2 * qb, kb), 1)

        def cond(carry):
            j, run, _ = carry
            return jnp.logical_and(j >= 0, jnp.max(run) > SB_LOG_ZERO)

        def body(carry):
            j, run, acc = carry
            start = pl.multiple_of(j * kb, kb)
            fresh = start + s_loc < covered
            log_beta, lk = gates(scores(q2, lanes, start, kb), fresh)
            within, total = block_sums(lk)
            acc = acc + attend([weights(log_beta, within, run)], lanes, start)
            return j - 1, run + total, acc

        return lax.while_loop(cond, body, (j0, run, acc))[2]

    def window(rows, nblk):
        nslab = qb // rows
        units = [(p, h) for p in range(SB_PAIRS) for h in range(nslab)]
        win0 = [pl.multiple_of(jnp.maximum(i * qb + (h + 1) * rows - nblk * kb, 0), rows)
                for h in range(nslab)]
        r_loc = lax.broadcasted_iota(jnp.int32, (2 * rows, kb), 0) % rows
        s_loc = lax.broadcasted_iota(jnp.int32, (2 * rows, kb), 1)
        last_visible = [s_loc < r_loc + (i * qb + h * rows - win0[h] - (nblk - 1) * kb)
                        for h in range(nslab)]

        q2 = {(p, h): queries(pair_lanes[p], h * rows, rows) for p, h in units}
        z = {u: scores(q2[u], pair_lanes[u[0]], win0[u[1]], nblk * kb) for u in units}
        gate = {(u, w): gates(z[u][:, w * kb:(w + 1) * kb],
                              last_visible[u[1]] if w == nblk - 1 else None)
                for u in units for w in range(nblk)}
        sums = {uw: block_sums(gate[uw][1]) for uw in gate}
        run, acc = {}, {}
        for u in units:
            run[u] = jnp.zeros((2 * rows, kb), F32)
            a_blocks = [None] * nblk
            for w in reversed(range(nblk)):
                a_blocks[w] = weights(gate[u, w][0], sums[u, w][0], run[u])
                run[u] = run[u] + sums[u, w][1]
            acc[u] = attend(a_blocks, pair_lanes[u[0]], win0[u[1]])
        left = functools.reduce(jnp.maximum, run.values())
        left = jnp.max(jnp.max(left, axis=0, keepdims=True)[:, :1])

        def done():
            return tuple(jnp.concatenate([acc[p, h] for h in range(nslab)], axis=0)
                         for p in range(SB_PAIRS))

        def rest():
            covered = jnp.concatenate([jnp.full((rows, kb), 1, jnp.int32) * win0[h]
                                       for _ in range(2) for h in range(nslab)], axis=0)
            outs = []
            for p, lanes in enumerate(pair_lanes):
                run_p = jnp.concatenate([run[p, h][hd * rows:(hd + 1) * rows]
                                         for hd in range(2) for h in range(nslab)], axis=0)
                outs.append(sweep(lanes, j0, covered, run_p, done()[p]))
            return tuple(outs)

        j0 = (win0[-1] + kb - 1) // kb - 1
        more = jnp.logical_and(j0 >= 0, left > SB_LOG_ZERO)
        outs = lax.cond(more, rest, done)
        for p, lanes in enumerate(pair_lanes):
            o_ref[0, :, lanes] = outs[p].astype(o_ref.dtype)

    full = i * qb >= (SB_SLAB_BLOCKS - 1) * kb
    pl.when(full)(lambda: window(SB_SLAB, SB_SLAB_BLOCKS))
    pl.when(jnp.logical_not(full))(lambda: window(qb, 1))


def _sb_sum_matrix():
    j = jnp.arange(SB_KB)
    later = j[:, None] > j[None, :]
    half = jnp.concatenate([later, jnp.ones((SB_KB, SB_KB), bool)], axis=1)
    return jnp.concatenate([half, half], axis=0).astype(BF16)


def _sb_attn(q, kv, bsz, seq):
    q3 = q.reshape(bsz, seq, D_MODEL)
    kv3 = kv.reshape(bsz, seq, 2 * D_MODEL)
    width = SB_PAIRS * LANES
    groups = D_MODEL // width
    step_rows = SB_STEP_BLOCKS * SB_QB
    return pl.pallas_call(
        _sb_kernel,
        grid=(bsz, groups, seq // step_rows),
        in_specs=[
            pl.BlockSpec((1, step_rows, width), lambda b, p, i: (b, i, p)),
            pl.BlockSpec((1, seq, width), lambda b, p, i: (b, 0, p)),
            pl.BlockSpec((1, seq, width), lambda b, p, i: (b, 0, groups + p)),
            pl.BlockSpec((2 * SB_KB, 2 * SB_KB), lambda b, p, i: (0, 0)),
        ],
        out_specs=pl.BlockSpec((1, step_rows, width), lambda b, p, i: (b, i, p)),
        out_shape=jax.ShapeDtypeStruct((bsz, seq, D_MODEL), BF16),
        compiler_params=pltpu.CompilerParams(
            dimension_semantics=("parallel", "parallel", "arbitrary"),
            vmem_limit_bytes=VMEM_LIMIT),
        name="sb_attn",
    )(q3, kv3, kv3, _sb_sum_matrix())


def kernel(x, ln_g, ln_b, ffn_w_up, ffn_w_down, gla_w_in, gla_w_gk, gla_b_gk, gla_norm_g,
           gla_w_out, sb_w_kv, sb_w_q, sb_w_out):
    bsz, seq, d = x.shape
    assert d == D_MODEL and seq % (SB_STEP_BLOCKS * SB_QB) == 0 and (bsz * seq) % BLK_TM == 0
    assert SB_QB == SB_KB and SB_QB % SB_SLAB == 0 and DEPTH == 2 and N_A_LAYERS == 1
    x = x.reshape(bsz * seq, d)
    w_up, w_down = ffn_w_up, ffn_w_down
    assert FFN_WU_COLS % FFN_TF == 0 and 2 * D_FF % FFN_W_STEPS == 0 and FFN_WD_ROWS % 16 == 0

    w_in = gla_w_in[0].astype(BF16)
    w_low = jnp.pad(w_in[:, -GATE_RANK:], ((0, 0), (0, LANES - GATE_RANK)))
    w_gk = jnp.pad(gla_w_gk[0], ((0, LANES - GATE_RANK), (0, 0))).astype(BF16)
    x, y = _block(w_up, w_down, 0, 0, ln_g[0, 0], ln_b[0, 0], x=x,
                  post=("gla", w_in, w_low, w_gk, gla_b_gk[0], gla_norm_g[0], seq))
    x, kv = _block(w_up, w_down, 0, 1, ln_g[0, 2], ln_b[0, 2],
                   pre=(y, x, gla_w_out[0].astype(BF16), ln_g[0, 1], ln_b[0, 1]),
                   post=("matmul", sb_w_kv.astype(BF16)))

    x, q = _block(w_up, w_down, 1, 0, ln_g[1, 0], ln_b[1, 0], x=x,
                  post=("matmul", sb_w_q[0].astype(BF16)))
    y = _sb_attn(q, kv, bsz, seq).reshape(bsz * seq, D_MODEL)
    (x,) = _block(w_up, w_down, 1, 1, ln_g[1, 2], ln_b[1, 2],
                  pre=(y, x, sb_w_out[0].astype(BF16), ln_g[1, 1], ln_b[1, 1]))
    return x.reshape(bsz, seq, d)
```
